```python
import math
import jax, jax.numpy as jnp
from jax import lax
import numpy as np

D_MODEL = 2048
BATCH = 8
SEQ = 4096
DEPTH = 1

N_META = 16
BLOCK_Q = 128
SB_HEADS = 8
SB_HEAD_DIM = 128
SB_WIDTH = SB_HEADS * SB_HEAD_DIM
DIFF_HEADS = 8
DIFF_QK_DIM = 64
DIFF_V_DIM = 2 * DIFF_QK_DIM
DIFF_QK_WIDTH = DIFF_HEADS * 2 * DIFF_QK_DIM
DIFF_WIDTH = DIFF_HEADS * DIFF_V_DIM
MIX_WIDTH = SB_WIDTH + DIFF_WIDTH
IN_SIZES = (SB_WIDTH, SB_WIDTH, SB_WIDTH, SB_WIDTH,
            DIFF_QK_WIDTH, DIFF_QK_WIDTH, DIFF_WIDTH, DIFF_WIDTH)
IN_WIDTH = 4 * SB_WIDTH + 2 * DIFF_QK_WIDTH + 2 * DIFF_WIDTH
ROPE_THETA = 500000.0
ROT_DIM = DIFF_QK_DIM // 4
RMS_EPS = 1e-6
SUBLN_EPS = 1e-5

kernel_name = "hybrid_stickbreak_diffattn_layer"


def rms_norm(x, w, eps=RMS_EPS):
    xf = x.astype(jnp.float32)
    y = xf * lax.rsqrt(jnp.mean(xf * xf, axis=-1, keepdims=True) + eps)
    return (y * w.astype(jnp.float32)).astype(x.dtype)


def lambda_init_for(layer):
    return 0.8 - 0.6 * math.exp(-0.3 * layer)


def partial_rope(x, cos, sin):
    half = ROT_DIM // 2
    c = cos[None, :, None, None, :]
    s = sin[None, :, None, None, :]
    x1 = x[..., :half]
    x2 = x[..., half:ROT_DIM]
    return jnp.concatenate([x1 * c - x2 * s, x2 * c + x1 * s, x[..., ROT_DIM:]], axis=-1)


def stick_breaking_block(q, k, v, q_pos, k_pos):
    z = jnp.einsum('bhqd,bhkd->bhqk', q, k) / math.sqrt(q.shape[-1])
    mask = k_pos[None, :] < q_pos[:, None]
    log_beta = jax.nn.log_sigmoid(z)
    log_keep = jnp.where(mask, jax.nn.log_sigmoid(-z), 0.0)
    tail = lax.cumsum(log_keep, axis=3, reverse=True) - log_keep
    a = jnp.where(mask, jnp.exp(log_beta + tail), 0.0)
    return jnp.einsum('bhqk,bhkd->bhqd', a, v)


def diff_attention_block(q, k, v, lam, q_pos, k_pos):
    s = jnp.einsum('bhcqd,bhckd->bhcqk', q, k) / math.sqrt(q.shape[-1])
    mask = k_pos[None, :] <= q_pos[:, None]
    p = jax.nn.softmax(jnp.where(mask, s, -jnp.inf), axis=-1)
    w = p[:, :, 0] - lam * p[:, :, 1]
    return jnp.einsum('bhqk,bhkd->bhqd', w, v)


def setup_inputs(seed: int = 0) -> dict:
    key = jax.random.key(seed)
    ks = jax.random.split(key, 13)
    f32 = jnp.float32
    x = jax.random.normal(ks[0], (BATCH, SEQ, D_MODEL), f32)
    meta = jax.random.normal(ks[1], (N_META, D_MODEL), f32)
    norm_w = 1.0 + 0.02 * jax.random.normal(ks[2], (DEPTH, D_MODEL), f32)
    w_in = jax.random.normal(ks[3], (DEPTH, D_MODEL, IN_WIDTH), f32) * D_MODEL ** -0.5
    q_norm_w = 1.0 + 0.02 * jax.random.normal(ks[4], (DEPTH, DIFF_QK_DIM), f32)
    k_norm_w = 1.0 + 0.02 * jax.random.normal(ks[5], (DEPTH, DIFF_QK_DIM), f32)
    lambda_q1 = 0.1 * jax.random.normal(ks[6], (DEPTH, DIFF_QK_DIM), f32)
    lambda_k1 = 0.1 * jax.random.normal(ks[7], (DEPTH, DIFF_QK_DIM), f32)
    lambda_q2 = 0.1 * jax.random.normal(ks[8], (DEPTH, DIFF_QK_DIM), f32)
    lambda_k2 = 0.1 * jax.random.normal(ks[9], (DEPTH, DIFF_QK_DIM), f32)
    subln_w = 1.0 + 0.02 * jax.random.normal(ks[10], (DEPTH, DIFF_V_DIM), f32)
    w_out = jax.random.normal(ks[11], (DEPTH, MIX_WIDTH, D_MODEL), f32) * MIX_WIDTH ** -0.5
    return {"x": x, "meta": meta, "norm_w": norm_w, "w_in": w_in,
            "q_norm_w": q_norm_w, "k_norm_w": k_norm_w,
            "lambda_q1": lambda_q1, "lambda_k1": lambda_k1,
            "lambda_q2": lambda_q2, "lambda_k2": lambda_k2,
            "subln_w": subln_w, "w_out": w_out}


def reference(x, meta, norm_w, w_in, q_norm_w, k_norm_w, lambda_q1, lambda_k1,
              lambda_q2, lambda_k2, subln_w, w_out):
    f32 = jnp.float32
    b, seq = x.shape[0], x.shape[1]
    h = jnp.concatenate(
        [jnp.broadcast_to(meta[None].astype(x.dtype), (b, N_META, D_MODEL)), x], axis=1)
    t_all = h.shape[1]
    pos = jnp.arange(t_all, dtype=f32)
    inv_freq = ROPE_THETA ** (-jnp.arange(0, ROT_DIM, 2, dtype=f32) / ROT_DIM)
    ang = pos[:, None] * inv_freq[None, :]
    cos, sin = jnp.cos(ang), jnp.sin(ang)
    split_at = list(np.cumsum(IN_SIZES)[:-1])
    real_blocks = [(N_META + i * BLOCK_Q, N_META + (i + 1) * BLOCK_Q)
                   for i in range(seq // BLOCK_Q)]

    for l in range(DEPTH):
        last = l == DEPTH - 1
        q_start = N_META if last else 0
        blocks = ([] if last else [(0, N_META)]) + real_blocks
        lam_init = lambda_init_for(l)

        u = rms_norm(h, norm_w[l])
        proj = jnp.einsum('btd,de->bte', u, w_in[l])
        sb_q, sb_k, sb_v, sb_g, df_q, df_k, df_v, df_g = jnp.split(proj, split_at, axis=-1)

        def heads(t, n, d):
            return t.reshape(b, t.shape[1], n, d).transpose(0, 2, 1, 3).astype(f32)

        sbq = heads(sb_q[:, q_start:], SB_HEADS, SB_HEAD_DIM)
        sbk = heads(sb_k, SB_HEADS, SB_HEAD_DIM)
        sbv = heads(sb_v, SB_HEADS, SB_HEAD_DIM)

        dq = df_q.reshape(b, t_all, DIFF_HEADS, 2, DIFF_QK_DIM).astype(f32)
        dk = df_k.reshape(b, t_all, DIFF_HEADS, 2, DIFF_QK_DIM).astype(f32)
        dq = partial_rope(rms_norm(dq, q_norm_w[l]), cos, sin).transpose(0, 2, 3, 1, 4)
        dk = partial_rope(rms_norm(dk, k_norm_w[l]), cos, sin).transpose(0, 2, 3, 1, 4)
        dq = dq[:, :, :, q_start:]
        dv = heads(df_v, DIFF_HEADS, DIFF_V_DIM)
        lam = (jnp.exp(jnp.sum(lambda_q1[l].astype(f32) * lambda_k1[l].astype(f32)))
               - jnp.exp(jnp.sum(lambda_q2[l].astype(f32) * lambda_k2[l].astype(f32)))
               + lam_init)

        sb_outs, df_outs = [], []
        for (a, e) in blocks:
            q_pos = jnp.arange(a, e)
            k_pos = jnp.arange(e)
            qa, qe = a - q_start, e - q_start
            sb_outs.append(stick_breaking_block(
                sbq[:, :, qa:qe], sbk[:, :, :e], sbv[:, :, :e], q_pos, k_pos))
            df_outs.append(diff_attention_block(
                dq[:, :, :, qa:qe], dk[:, :, :, :e], dv[:, :, :e], lam, q_pos, k_pos))

        t_q = t_all - q_start
        sb_o = jnp.concatenate(sb_outs, axis=2).transpose(0, 2, 1, 3).reshape(b, t_q, SB_WIDTH)
        df_o = jnp.concatenate(df_outs, axis=2).transpose(0, 2, 1, 3)
        df_o = (rms_norm(df_o, subln_w[l], SUBLN_EPS) * (1.0 - lam_init)).reshape(b, t_q, DIFF_WIDTH)

        mixed = jnp.concatenate(
            [sb_o * jax.nn.silu(sb_g[:, q_start:].astype(f32)),
             df_o * jax.nn.silu(df_g[:, q_start:].astype(f32))], axis=-1).astype(h.dtype)
        y = jnp.einsum('bte,ed->btd', mixed, w_out[l])
        h = jnp.concatenate([h[:, :q_start], h[:, q_start:] + y], axis=1)

    return h[:, N_META:]
```

```python
import functools
import math

import jax
import jax.numpy as jnp
from jax import lax
from jax.experimental import pallas as pl
from jax.experimental.pallas import tpu as pltpu

N_META = 16
HEAD = 128
N_HEADS = 8
SEG = N_HEADS * HEAD
N_SEG = 8
SEG_DF_Q, SEG_DF_K = 4, 5
QK_DIM = 64
ROT_HALF = 8
ROPE_THETA = 500000.0
RMS_EPS = 1e-6
SUBLN_EPS = 1e-5
LAMBDA_INIT = 0.8 - 0.6 * math.exp(-0.3 * 0)
SB_SCALE = 1.0 / math.sqrt(HEAD)
DF_SCALE = 1.0 / math.sqrt(QK_DIM)

LANES = 128
V7X_VMEM_LIMIT = 56 * 1024 * 1024

F32 = jnp.float32
BF16 = jnp.bfloat16

_NT = (((1,), (1,)), ((), ()))
_TN = (((0,), (0,)), ((), ()))


def _inproj_kernel(x_ref, nw_ref, w_ref, cw_ref, cos_ref, s1_ref, s2_ref, o_ref, u_ref):
    j = pl.program_id(1)

    @pl.when(j == 0)
    def _():
        x = x_ref[...]
        ms = jnp.mean(x * x, axis=-1, keepdims=True)
        u_ref[...] = (x * lax.rsqrt(ms + RMS_EPS) * nw_ref[...]).astype(BF16)

    acc = jnp.dot(u_ref[...], w_ref[...], preferred_element_type=F32)
    is_qk = jnp.logical_or(j == SEG_DF_Q, j == SEG_DF_K)

    @pl.when(jnp.logical_not(is_qk))
    def _():
        o_ref[...] = acc.astype(o_ref.dtype)

    @pl.when(is_qk)
    def _():
        first = lax.broadcasted_iota(jnp.int32, (1, LANES), 1) < QK_DIM
        cos, s1, s2 = cos_ref[...], s1_ref[...], s2_ref[...]
        for c in range(acc.shape[1] // LANES):
            cols = slice(c * LANES, (c + 1) * LANES)
            y = acc[:, cols]
            y2 = y * y
            sa = jnp.sum(jnp.where(first, y2, 0.0), axis=1, keepdims=True)
            sb = jnp.sum(jnp.where(first, 0.0, y2), axis=1, keepdims=True)
            ms = jnp.where(first, sa, sb) * (1.0 / QK_DIM)
            yn = y * lax.rsqrt(ms + RMS_EPS) * cw_ref[:, cols]
            out = (yn * cos + pltpu.roll(yn, LANES - ROT_HALF, 1) * s1
                   + pltpu.roll(yn, ROT_HALF, 1) * s2)
            o_ref[:, cols] = out.astype(o_ref.dtype)


def _inproj(x2d, norm_w, w_bf16, col_w, cos_t, s1_t, s2_t, *, rows_per_seq, tm):
    m, d = x2d.shape
    n = w_bf16.shape[1]
    tn = SEG
    n_pos_tiles = rows_per_seq // tm
    pos_map = lambda i, j: (i % n_pos_tiles, 0)
    return pl.pallas_call(
        _inproj_kernel,
        grid=(m // tm, n // tn),
        in_specs=[
            pl.BlockSpec((tm, d), lambda i, j: (i, 0)),
            pl.BlockSpec((1, d), lambda i, j: (0, 0)),
            pl.BlockSpec((d, tn), lambda i, j: (0, j)),
            pl.BlockSpec((1, tn), lambda i, j: (0, j)),
            pl.BlockSpec((tm, LANES), pos_map),
            pl.BlockSpec((tm, LANES), pos_map),
            pl.BlockSpec((tm, LANES), pos_map),
        ],
        out_specs=pl.BlockSpec((tm, tn), lambda i, j: (i, j)),
        out_shape=jax.ShapeDtypeStruct((m, n), BF16),
        scratch_shapes=[pltpu.VMEM((tm, d), BF16)],
        compiler_params=pltpu.CompilerParams(
            dimension_semantics=("parallel", "arbitrary"),
            vmem_limit_bytes=V7X_VMEM_LIMIT),
        name="inproj",
    )(x2d, norm_w, w_bf16, col_w, cos_t, s1_t, s2_t)


def _sb_block(q, k, v, tri, carry, acc, valid):
    z = lax.dot_general(k, q, _NT, preferred_element_type=F32) * SB_SCALE
    sp = jnp.log1p(jnp.exp(-jnp.abs(z)))
    log_beta = jnp.minimum(z, 0.0) - sp
    log_keep = log_beta - z
    if valid is not None:
        log_keep = jnp.where(valid, log_keep, 0.0)
    hi = log_keep.astype(BF16)
    lo = (log_keep - hi.astype(F32)).astype(BF16)
    tail = (jnp.dot(tri, hi, preferred_element_type=F32)
            + jnp.dot(tri, lo, preferred_element_type=F32))
    a = jnp.exp(log_beta + tail + carry)
    if valid is not None:
        a = jnp.where(valid, a, 0.0)
    acc = acc + lax.dot_general(v, a.astype(BF16), _TN, preferred_element_type=F32)
    carry = carry + tail[0:1, :] + log_keep[0:1, :]
    return carry, acc


def _silu(g):
    return g * (1.0 / (1.0 + jnp.exp(-g)))


def _sb_kernel(q_ref, k_ref, v_ref, g_ref, km_ref, vm_ref, tri_ref, o_ref, *, blk):
    i = pl.program_id(2)
    q = q_ref[...]
    tri = tri_ref[...]
    r = lax.broadcasted_iota(jnp.int32, (blk, blk), 0)
    c = lax.broadcasted_iota(jnp.int32, (blk, blk), 1)

    def rows(j):
        return pl.ds(pl.multiple_of(j * blk, blk), blk)

    carry = jnp.zeros((1, blk), F32)
    acc = jnp.zeros((HEAD, blk), F32)
    carry, acc = _sb_block(q, k_ref[rows(i), :], v_ref[rows(i), :], tri, carry, acc, r < c)

    def body(jj, st):
        j = i - 1 - jj
        return _sb_block(q, k_ref[rows(j), :], v_ref[rows(j), :], tri, st[0], st[1], None)

    carry, acc = lax.fori_loop(0, i, body, (carry, acc))
    carry, acc = _sb_block(q, km_ref[...], vm_ref[...], tri_ref[0:N_META, 0:N_META],
                           carry, acc, None)
    g = g_ref[...].astype(F32)
    o_ref[...] = (acc.T * _silu(g)).astype(o_ref.dtype)


def _df_block(q2, k, v, m, l, acc, valid):
    s = lax.dot_general(k, q2, _NT, preferred_element_type=F32)
    if valid is not None:
        s = jnp.where(valid, s, -jnp.inf)
    m_new = jnp.maximum(m, jnp.max(s, axis=0, keepdims=True))
    alpha = jnp.exp(m - m_new)
    p = jnp.exp(s - m_new)
    l = alpha * l + jnp.sum(p, axis=0, keepdims=True)
    acc = alpha * acc + lax.dot_general(v, p.astype(BF16), _TN, preferred_element_type=F32)
    return m_new, l, acc


def _df_kernel(q_ref, k_ref, v_ref, g_ref, km_ref, vm_ref, lq1_ref, lk1_ref, lq2_ref, lk2_ref,
               sw_ref, o_ref, *, blk):
    i = pl.program_id(2)
    qf = q_ref[...].astype(F32)
    first = lax.broadcasted_iota(jnp.int32, (1, LANES), 1) < QK_DIM
    q2 = jnp.concatenate([jnp.where(first, qf, 0.0), jnp.where(first, 0.0, qf)],
                         axis=0).astype(BF16)

    def rows(j):
        return pl.ds(pl.multiple_of(j * blk, blk), blk)

    km, vm = km_ref[...], vm_ref[...]
    s = lax.dot_general(km, q2, _NT, preferred_element_type=F32)
    m = jnp.max(s, axis=0, keepdims=True)
    p = jnp.exp(s - m)
    l = jnp.sum(p, axis=0, keepdims=True)
    acc = lax.dot_general(vm, p.astype(BF16), _TN, preferred_element_type=F32)

    def body(j, st):
        return _df_block(q2, k_ref[rows(j), :], v_ref[rows(j), :], st[0], st[1], st[2], None)

    m, l, acc = lax.fori_loop(0, i, body, (m, l, acc))
    r = lax.broadcasted_iota(jnp.int32, (blk, 2 * blk), 0)
    c = lax.broadcasted_iota(jnp.int32, (blk, 2 * blk), 1)
    c = jnp.where(c >= blk, c - blk, c)
    m, l, acc = _df_block(q2, k_ref[rows(i), :], v_ref[rows(i), :], m, l, acc, r <= c)

    lam = (jnp.exp(jnp.sum(lq1_ref[...] * lk1_ref[...], axis=1, keepdims=True))
           - jnp.exp(jnp.sum(lq2_ref[...] * lk2_ref[...], axis=1, keepdims=True))
           + LAMBDA_INIT)
    o = acc / l
    o = o[:, :blk] - lam * o[:, blk:]
    ms = jnp.mean(o * o, axis=0, keepdims=True)
    y = (o * lax.rsqrt(ms + SUBLN_EPS)).T * (sw_ref[...] * (1.0 - LAMBDA_INIT))
    g = g_ref[...].astype(F32)
    o_ref[...] = (y * _silu(g)).astype(o_ref.dtype)


def _attention(kernel_fn, proj, proj_meta, extra, extra_specs, *, seg0, blk, name):
    b, s, _ = proj.shape

    def col(seg):
        return lambda bi, h, i, seg=seg: (bi, i, seg * N_HEADS + h)

    def full(seg):
        return lambda bi, h, i, seg=seg: (bi, 0, seg * N_HEADS + h)

    def meta(seg):
        return lambda bi, h, i, seg=seg: (0, seg * N_HEADS + h)

    return pl.pallas_call(
        functools.partial(kernel_fn, blk=blk),
        grid=(b, N_HEADS, s // blk),
        in_specs=[
            pl.BlockSpec((None, blk, HEAD), col(seg0)),
            pl.BlockSpec((None, s, HEAD), full(seg0 + 1)),
            pl.BlockSpec((None, s, HEAD), full(seg0 + 2)),
            pl.BlockSpec((None, blk, HEAD), col(seg0 + 3)),
            pl.BlockSpec((N_META, HEAD), meta(seg0 + 1)),
            pl.BlockSpec((N_META, HEAD), meta(seg0 + 2)),
        ] + extra_specs,
        out_specs=pl.BlockSpec((None, blk, HEAD), lambda bi, h, i: (bi, i, h)),
        out_shape=jax.ShapeDtypeStruct((b, s, SEG), BF16),
        compiler_params=pltpu.CompilerParams(
            dimension_semantics=("parallel", "parallel", "arbitrary"),
            vmem_limit_bytes=V7X_VMEM_LIMIT),
        name=name,
    )(proj, proj, proj, proj, proj_meta, proj_meta, *extra)


def _outproj_kernel(sb_ref, df_ref, wsb_ref, wdf_ref, x_ref, o_ref):
    y = (jnp.dot(sb_ref[...], wsb_ref[...], preferred_element_type=F32)
         + jnp.dot(df_ref[...], wdf_ref[...], preferred_element_type=F32))
    o_ref[...] = x_ref[...] + y


def _outproj(sb_mixed, df_mixed, w_out_bf16, x2d, *, tm):
    m, d = x2d.shape
    const = lambda i: (0, 0)
    return pl.pallas_call(
        _outproj_kernel,
        grid=(m // tm,),
        in_specs=[
            pl.BlockSpec((tm, SEG), lambda i: (i, 0)),
            pl.BlockSpec((tm, SEG), lambda i: (i, 0)),
            pl.BlockSpec((SEG, d), lambda i: (0, 0)),
            pl.BlockSpec((SEG, d), lambda i: (1, 0)),
            pl.BlockSpec((tm, d), lambda i: (i, 0)),
        ],
        out_specs=pl.BlockSpec((tm, d), lambda i: (i, 0)),
        out_shape=jax.ShapeDtypeStruct((m, d), x2d.dtype),
        compiler_params=pltpu.CompilerParams(
            dimension_semantics=("parallel",),
            vmem_limit_bytes=V7X_VMEM_LIMIT),
        name="outproj",
    )(sb_mixed, df_mixed, w_out_bf16, w_out_bf16, x2d)


def _rope_tables(t_all):
    pos = jnp.arange(t_all, dtype=F32)
    inv_freq = ROPE_THETA ** (-jnp.arange(0, 2 * ROT_HALF, 2, dtype=F32) / (2 * ROT_HALF))
    ang = pos[:, None] * inv_freq[None, :]
    cos, sin = jnp.cos(ang), jnp.sin(ang)
    lane = jnp.arange(LANES) % QK_DIM
    pair = lane % ROT_HALF
    cos_t = jnp.where(lane < 2 * ROT_HALF, cos[:, pair], 1.0)
    s1_t = jnp.where(lane < ROT_HALF, -sin[:, pair], 0.0)
    s2_t = jnp.where((lane >= ROT_HALF) & (lane < 2 * ROT_HALF), sin[:, pair], 0.0)
    return cos_t, s1_t, s2_t


def kernel(x, meta, norm_w, w_in, q_norm_w, k_norm_w, lambda_q1, lambda_k1, lambda_q2,
           lambda_k2, subln_w, w_out):
    b, s, d = x.shape
    assert norm_w.shape[0] == 1, "single-layer problem"
    blk = 256
    tm = min(1024, s)
    assert s % tm == 0 and s % blk == 0

    w_in_b = w_in[0].astype(BF16)
    w_out_b = w_out[0].astype(BF16)
    col_w = jnp.ones((1, N_SEG * SEG), F32)
    col_w = col_w.at[0, SEG_DF_Q * SEG:(SEG_DF_Q + 1) * SEG].set(
        jnp.tile(q_norm_w[0] * DF_SCALE, SEG // QK_DIM))
    col_w = col_w.at[0, SEG_DF_K * SEG:(SEG_DF_K + 1) * SEG].set(
        jnp.tile(k_norm_w[0], SEG // QK_DIM))
    cos_t, s1_t, s2_t = _rope_tables(N_META + s)

    x2d = x.reshape(b * s, d)
    proj = _inproj(x2d, norm_w, w_in_b, col_w, cos_t[N_META:], s1_t[N_META:], s2_t[N_META:],
                   rows_per_seq=s, tm=tm).reshape(b, s, N_SEG * SEG)
    proj_meta = _inproj(meta, norm_w, w_in_b, col_w, cos_t[:N_META], s1_t[:N_META],
                        s2_t[:N_META], rows_per_seq=N_META, tm=N_META)

    tri = (jnp.arange(blk)[None, :] > jnp.arange(blk)[:, None]).astype(BF16)
    const2 = lambda bi, h, i: (0, 0)
    sb_mixed = _attention(_sb_kernel, proj, proj_meta, [tri],
                          [pl.BlockSpec((blk, blk), const2)], seg0=0, blk=blk, name="sb_attn")
    vec = pl.BlockSpec((1, QK_DIM), const2)
    df_mixed = _attention(_df_kernel, proj, proj_meta,
                          [lambda_q1, lambda_k1, lambda_q2, lambda_k2, subln_w],
                          [vec, vec, vec, vec, pl.BlockSpec((1, HEAD), const2)],
                          seg0=4, blk=blk, name="df_attn")
    out = _outproj(sb_mixed.reshape(b * s, SEG), df_mixed.reshape(b * s, SEG), w_out_b, x2d,
                   tm=min(512, s))
    return out.reshape(b, s, d)
```

```python
import functools
import math

import jax
import jax.numpy as jnp
from jax import lax
from jax.experimental import pallas as pl
from jax.experimental.pallas import tpu as pltpu

N_META = 16
HEAD = 128
N_HEADS = 8
SEG = N_HEADS * HEAD
N_SEG = 8
SEG_DF_Q, SEG_DF_K = 4, 5
QK_DIM = 64
ROT_HALF = 8
ROPE_THETA = 500000.0
RMS_EPS = 1e-6
SUBLN_EPS = 1e-5
LAMBDA_INIT = 0.8 - 0.6 * math.exp(-0.3 * 0)
SB_SCALE = 1.0 / math.sqrt(HEAD)
SB_DEAD_LOG2 = -130.0
DF_SCALE = 1.0 / math.sqrt(QK_DIM)
LOG2_E = math.log2(math.e)
SB_HEADS_PER_STEP = 4
DF_HEADS_PER_STEP = 4

LANES = 128
V7X_VMEM_LIMIT = 56 * 1024 * 1024

F32 = jnp.float32
BF16 = jnp.bfloat16

_NT = (((1,), (1,)), ((), ()))
_TN = (((0,), (0,)), ((), ()))


def _inproj_kernel(x_ref, nw_ref, w_ref, cw_ref, cos_ref, s1_ref, s2_ref, o_ref, u_ref):
    j = pl.program_id(1)

    @pl.when(j == 0)
    def _():
        x = x_ref[...]
        ms = jnp.mean(x * x, axis=-1, keepdims=True)
        u_ref[...] = (x * lax.rsqrt(ms + RMS_EPS) * nw_ref[...]).astype(BF16)

    acc = jnp.dot(u_ref[...], w_ref[...], preferred_element_type=F32)
    is_qk = jnp.logical_or(j == SEG_DF_Q, j == SEG_DF_K)

    @pl.when(jnp.logical_not(is_qk))
    def _():
        o_ref[...] = (acc * cw_ref[...]).astype(o_ref.dtype)

    @pl.when(is_qk)
    def _():
        first = lax.broadcasted_iota(jnp.int32, (1, LANES), 1) < QK_DIM
        cos, s1, s2 = cos_ref[...], s1_ref[...], s2_ref[...]
        for c in range(acc.shape[1] // LANES):
            cols = slice(c * LANES, (c + 1) * LANES)
            y = acc[:, cols]
            y2 = y * y
            sa = jnp.sum(jnp.where(first, y2, 0.0), axis=1, keepdims=True)
            sb = jnp.sum(jnp.where(first, 0.0, y2), axis=1, keepdims=True)
            ms = jnp.where(first, sa, sb) * (1.0 / QK_DIM)
            yn = y * lax.rsqrt(ms + RMS_EPS) * cw_ref[:, cols]
            out = (yn * cos + pltpu.roll(yn, LANES - ROT_HALF, 1) * s1
                   + pltpu.roll(yn, ROT_HALF, 1) * s2)
            o_ref[:, cols] = out.astype(o_ref.dtype)


def _inproj(x2d, norm_w, w_bf16, col_w, cos_t, s1_t, s2_t, *, rows_per_seq, tm):
    m, d = x2d.shape
    n = w_bf16.shape[1]
    tn = SEG
    n_pos_tiles = rows_per_seq // tm
    pos_map = lambda i, j: (i % n_pos_tiles, 0)
    return pl.pallas_call(
        _inproj_kernel,
        grid=(m // tm, n // tn),
        in_specs=[
            pl.BlockSpec((tm, d), lambda i, j: (i, 0)),
            pl.BlockSpec((1, d), lambda i, j: (0, 0)),
            pl.BlockSpec((d, tn), lambda i, j: (0, j)),
            pl.BlockSpec((1, tn), lambda i, j: (0, j)),
            pl.BlockSpec((tm, LANES), pos_map),
            pl.BlockSpec((tm, LANES), pos_map),
            pl.BlockSpec((tm, LANES), pos_map),
        ],
        out_specs=pl.BlockSpec((tm, tn), lambda i, j: (i, j)),
        out_shape=jax.ShapeDtypeStruct((m, n), BF16),
        scratch_shapes=[pltpu.VMEM((tm, d), BF16)],
        compiler_params=pltpu.CompilerParams(
            dimension_semantics=("parallel", "arbitrary"),
            vmem_limit_bytes=V7X_VMEM_LIMIT),
        name="inproj",
    )(x2d, norm_w, w_bf16, col_w, cos_t, s1_t, s2_t)


def _silu(g):
    return g * (1.0 / (1.0 + jnp.exp(-g)))


def _head_cols(h):
    return slice(h * HEAD, (h + 1) * HEAD)


def _fill_vt(v_ref, vt_ref, *, blk, heads):
    for h in range(heads):
        for j in range(v_ref.shape[0] // blk):
            vt_ref[h, j] = v_ref[j * blk:(j + 1) * blk, _head_cols(h)].T


def _sb_kernel(q_ref, k_ref, v_ref, g_ref, km_ref, vm_ref, tri_ref, o_ref, vt_ref, carry_ref,
               acc_ref, *, blk, heads):
    i = pl.program_id(2)
    hs = range(heads)

    @pl.when(i == 0)
    def _():
        _fill_vt(v_ref, vt_ref, blk=blk, heads=heads)

    def block(keys, weighted_values, tri, valid):
        zs = [lax.dot_general(keys(h), q_ref[:, _head_cols(h)], _NT, preferred_element_type=F32)
              for h in hs]
        log_betas, log_keeps, his, los = [], [], [], []
        for z in zs:
            sp = jnp.log2(1.0 + jnp.exp2(-jnp.abs(z)))
            log_beta = jnp.minimum(z, 0.0) - sp
            log_keep = log_beta - z
            if valid is not None:
                log_keep = jnp.where(valid, log_keep, 0.0)
            hi = log_keep.astype(BF16)
            log_betas.append(log_beta)
            log_keeps.append(log_keep)
            his.append(hi)
            los.append((log_keep - hi.astype(F32)).astype(BF16))
        tails = [jnp.dot(tri, his[h], preferred_element_type=F32)
                 + jnp.dot(tri, los[h], preferred_element_type=F32) for h in hs]
        probs = []
        for h in hs:
            a = jnp.exp2(log_betas[h] + tails[h] + carry_ref[h])
            if valid is not None:
                a = jnp.where(valid, a, 0.0)
            probs.append(a.astype(BF16))
        upd = [weighted_values(h, probs[h]) for h in hs]
        for h in hs:
            acc_ref[h] += upd[h]
            carry_ref[h] += tails[h][0:1, :] + log_keeps[h][0:1, :]

    def real_block(j, valid):
        block(lambda h: k_ref[pl.ds(pl.multiple_of(j * blk, blk), blk), _head_cols(h)],
              lambda h, a: jnp.dot(vt_ref[h, j], a, preferred_element_type=F32),
              tri_ref[...], valid)

    def live():
        return jnp.max(carry_ref[...]) > SB_DEAD_LOG2

    carry_ref[...] = jnp.zeros_like(carry_ref)
    acc_ref[...] = jnp.zeros_like(acc_ref)
    r = lax.broadcasted_iota(jnp.int32, (blk, blk), 0)
    c = lax.broadcasted_iota(jnp.int32, (blk, blk), 1)
    real_block(i, r < c)

    def older(state):
        jj, _ = state
        real_block(i - 1 - jj, None)
        return jj + 1, jnp.logical_and(jj + 1 < i, live())

    lax.while_loop(lambda state: state[1], older,
                   (jnp.int32(0), jnp.logical_and(i > 0, live())))

    @pl.when(live())
    def _():
        block(lambda h: km_ref[:, _head_cols(h)],
              lambda h, a: lax.dot_general(vm_ref[:, _head_cols(h)], a, _TN,
                                           preferred_element_type=F32),
              tri_ref[0:N_META, 0:N_META], None)

    for h in hs:
        g = g_ref[:, _head_cols(h)].astype(F32)
        o_ref[:, _head_cols(h)] = (acc_ref[h].T * _silu(g)).astype(o_ref.dtype)


def _df_kernel(q_ref, k_ref, v_ref, g_ref, km_ref, vm_ref, lq1_ref, lk1_ref, lq2_ref, lk2_ref,
               sw_ref, o_ref, q2_ref, vt_ref, m_ref, l_ref, acc_ref, *, blk, heads):
    i = pl.program_id(2)
    hs = range(heads)

    @pl.when(i == 0)
    def _():
        _fill_vt(v_ref, vt_ref, blk=blk, heads=heads)

    first = lax.broadcasted_iota(jnp.int32, (1, LANES), 1) < QK_DIM
    for h in hs:
        qf = q_ref[:, _head_cols(h)].astype(F32)
        q2_ref[h, 0:blk, :] = jnp.where(first, qf, 0.0).astype(BF16)
        q2_ref[h, blk:2 * blk, :] = jnp.where(first, 0.0, qf).astype(BF16)
    m_ref[...] = jnp.full_like(m_ref, -jnp.inf)
    l_ref[...] = jnp.zeros_like(l_ref)
    acc_ref[...] = jnp.zeros_like(acc_ref)

    def block(j0, n_blocks, diag_row, with_meta):
        off = N_META if with_meta else 0
        size = off + n_blocks * blk
        rows = pl.ds(pl.multiple_of(j0 * blk, blk), n_blocks * blk)

        def keys(h):
            k = k_ref[rows, _head_cols(h)]
            return jnp.concatenate([km_ref[:, _head_cols(h)], k], axis=0) if with_meta else k

        scores = [lax.dot_general(keys(h), q2_ref[h], _NT, preferred_element_type=F32)
                  for h in hs]
        if diag_row is not None:
            r = lax.broadcasted_iota(jnp.int32, (size, 2 * blk), 0)
            c = lax.broadcasted_iota(jnp.int32, (size, 2 * blk), 1)
            valid = r - (off + diag_row) <= jnp.where(c >= blk, c - blk, c)
        alphas, probs = [], []
        for h in hs:
            s = scores[h] if diag_row is None else jnp.where(valid, scores[h], -jnp.inf)
            m_new = jnp.maximum(m_ref[h], jnp.max(s, axis=0, keepdims=True))
            alpha = jnp.exp2(m_ref[h] - m_new)
            p = jnp.exp2(s - m_new)
            m_ref[h] = m_new
            l_ref[h] = alpha * l_ref[h] + jnp.sum(p, axis=0, keepdims=True)
            alphas.append(alpha)
            probs.append(p.astype(BF16))
        upd = []
        for h in hs:
            u = sum(jnp.dot(vt_ref[h, j0 + n], probs[h][off + n * blk:off + (n + 1) * blk],
                            preferred_element_type=F32) for n in range(n_blocks))
            if with_meta:
                u = u + lax.dot_general(vm_ref[:, _head_cols(h)], probs[h][0:off], _TN,
                                        preferred_element_type=F32)
            upd.append(u)
        for h in hs:
            acc_ref[h] = alphas[h] * acc_ref[h] + upd[h]

    @pl.loop(0, i // 2)
    def _(t):
        block(2 * t, 2, None, False)

    @pl.when(i % 2 == 1)
    def _():
        block(i - 1, 2, blk, True)

    @pl.when(i % 2 == 0)
    def _():
        block(i, 1, 0, True)

    lam = (jnp.exp(jnp.sum(lq1_ref[...] * lk1_ref[...], axis=1, keepdims=True))
           - jnp.exp(jnp.sum(lq2_ref[...] * lk2_ref[...], axis=1, keepdims=True))
           + LAMBDA_INIT)
    for h in hs:
        o = acc_ref[h] / l_ref[h]
        o = o[:, :blk] - lam * o[:, blk:]
        ms = jnp.mean(o * o, axis=0, keepdims=True)
        y = (o * lax.rsqrt(ms + SUBLN_EPS)).T * (sw_ref[...] * (1.0 - LAMBDA_INIT))
        g = g_ref[:, _head_cols(h)].astype(F32)
        o_ref[:, _head_cols(h)] = (y * _silu(g)).astype(o_ref.dtype)


def _attention(kernel_fn, proj, proj_meta, extra, extra_specs, scratch, *, seg0, blk, heads,
               name):
    b, s, _ = proj.shape
    groups = N_HEADS // heads
    width = heads * HEAD

    def col(seg):
        return lambda bi, h, i, seg=seg: (bi, i, seg * groups + h)

    def full(seg):
        return lambda bi, h, i, seg=seg: (bi, 0, seg * groups + h)

    def meta(seg):
        return lambda bi, h, i, seg=seg: (0, seg * groups + h)

    return pl.pallas_call(
        functools.partial(kernel_fn, blk=blk, heads=heads),
        grid=(b, groups, s // blk),
        in_specs=[
            pl.BlockSpec((None, blk, width), col(seg0)),
            pl.BlockSpec((None, s, width), full(seg0 + 1)),
            pl.BlockSpec((None, s, width), full(seg0 + 2)),
            pl.BlockSpec((None, blk, width), col(seg0 + 3)),
            pl.BlockSpec((N_META, width), meta(seg0 + 1)),
            pl.BlockSpec((N_META, width), meta(seg0 + 2)),
        ] + extra_specs,
        out_specs=pl.BlockSpec((None, blk, width), lambda bi, h, i: (bi, i, h)),
        out_shape=jax.ShapeDtypeStruct((b, s, SEG), BF16),
        scratch_shapes=scratch,
        compiler_params=pltpu.CompilerParams(
            dimension_semantics=("parallel", "parallel", "arbitrary"),
            vmem_limit_bytes=V7X_VMEM_LIMIT),
        name=name,
    )(proj, proj, proj, proj, proj_meta, proj_meta, *extra)


def _outproj_kernel(sb_ref, df_ref, wsb_ref, wdf_ref, x_ref, o_ref):
    y = (jnp.dot(sb_ref[...], wsb_ref[...], preferred_element_type=F32)
         + jnp.dot(df_ref[...], wdf_ref[...], preferred_element_type=F32))
    o_ref[...] = x_ref[...] + y


def _outproj(sb_mixed, df_mixed, w_out_bf16, x2d, *, tm):
    m, d = x2d.shape
    const = lambda i: (0, 0)
    return pl.pallas_call(
        _outproj_kernel,
        grid=(m // tm,),
        in_specs=[
            pl.BlockSpec((tm, SEG), lambda i: (i, 0)),
            pl.BlockSpec((tm, SEG), lambda i: (i, 0)),
            pl.BlockSpec((SEG, d), lambda i: (0, 0)),
            pl.BlockSpec((SEG, d), lambda i: (1, 0)),
            pl.BlockSpec((tm, d), lambda i: (i, 0)),
        ],
        out_specs=pl.BlockSpec((tm, d), lambda i: (i, 0)),
        out_shape=jax.ShapeDtypeStruct((m, d), x2d.dtype),
        compiler_params=pltpu.CompilerParams(
            dimension_semantics=("parallel",),
            vmem_limit_bytes=V7X_VMEM_LIMIT),
        name="outproj",
    )(sb_mixed, df_mixed, w_out_bf16, w_out_bf16, x2d)


def _rope_tables(t_all):
    pos = jnp.arange(t_all, dtype=F32)
    inv_freq = ROPE_THETA ** (-jnp.arange(0, 2 * ROT_HALF, 2, dtype=F32) / (2 * ROT_HALF))
    ang = pos[:, None] * inv_freq[None, :]
    cos, sin = jnp.cos(ang), jnp.sin(ang)
    lane = jnp.arange(LANES) % QK_DIM
    pair = lane % ROT_HALF
    cos_t = jnp.where(lane < 2 * ROT_HALF, cos[:, pair], 1.0)
    s1_t = jnp.where(lane < ROT_HALF, -sin[:, pair], 0.0)
    s2_t = jnp.where((lane >= ROT_HALF) & (lane < 2 * ROT_HALF), sin[:, pair], 0.0)
    return cos_t, s1_t, s2_t


def kernel(x, meta, norm_w, w_in, q_norm_w, k_norm_w, lambda_q1, lambda_k1, lambda_q2,
           lambda_k2, subln_w, w_out):
    b, s, d = x.shape
    assert norm_w.shape[0] == 1, "single-layer problem"
    blk = 256
    tm = min(1024, s)
    tm_out = min(512, s)
    assert s % tm == 0 and s % blk == 0 and s % tm_out == 0

    w_in_b = w_in[0].astype(BF16)
    w_out_b = w_out[0].astype(BF16)
    col_w = jnp.ones((1, N_SEG * SEG), F32).at[0, 0:SEG].set(SB_SCALE * LOG2_E)
    col_w = col_w.at[0, SEG_DF_Q * SEG:(SEG_DF_Q + 1) * SEG].set(
        jnp.tile(q_norm_w[0] * (DF_SCALE * LOG2_E), SEG // QK_DIM))
    col_w = col_w.at[0, SEG_DF_K * SEG:(SEG_DF_K + 1) * SEG].set(
        jnp.tile(k_norm_w[0], SEG // QK_DIM))
    cos_t, s1_t, s2_t = _rope_tables(N_META + s)

    x2d = x.reshape(b * s, d)
    proj = _inproj(x2d, norm_w, w_in_b, col_w, cos_t[N_META:], s1_t[N_META:], s2_t[N_META:],
                   rows_per_seq=s, tm=tm).reshape(b, s, N_SEG * SEG)
    proj_meta = _inproj(meta, norm_w, w_in_b, col_w, cos_t[:N_META], s1_t[:N_META],
                        s2_t[:N_META], rows_per_seq=N_META, tm=N_META)

    tri = (jnp.arange(blk)[None, :] > jnp.arange(blk)[:, None]).astype(BF16)
    const2 = lambda bi, h, i: (0, 0)
    sb_mixed = _attention(_sb_kernel, proj, proj_meta, [tri],
                          [pl.BlockSpec((blk, blk), const2)],
                          [pltpu.VMEM((SB_HEADS_PER_STEP, s // blk, HEAD, blk), BF16),
                           pltpu.VMEM((SB_HEADS_PER_STEP, 1, blk), F32),
                           pltpu.VMEM((SB_HEADS_PER_STEP, HEAD, blk), F32)],
                          seg0=0, blk=blk, heads=SB_HEADS_PER_STEP, name="sb_attn")
    vec = pl.BlockSpec((1, QK_DIM), const2)
    df_mixed = _attention(_df_kernel, proj, proj_meta,
                          [lambda_q1, lambda_k1, lambda_q2, lambda_k2, subln_w],
                          [vec, vec, vec, vec, pl.BlockSpec((1, HEAD), const2)],
                          [pltpu.VMEM((DF_HEADS_PER_STEP, 2 * blk, HEAD), BF16),
                           pltpu.VMEM((DF_HEADS_PER_STEP, s // blk, HEAD, blk), BF16),
                           pltpu.VMEM((DF_HEADS_PER_STEP, 1, 2 * blk), F32),
                           pltpu.VMEM((DF_HEADS_PER_STEP, 1, 2 * blk), F32),
                           pltpu.VMEM((DF_HEADS_PER_STEP, HEAD, 2 * blk), F32)],
                          seg0=4, blk=blk, heads=DF_HEADS_PER_STEP, name="df_attn")
    out = _outproj(sb_mixed.reshape(b * s, SEG), df_mixed.reshape(b * s, SEG), w_out_b, x2d,
                   tm=tm_out)
    return out.reshape(b, s, d)
```

```python
import functools
import math

import jax
import jax.numpy as jnp
from jax import lax
from jax.experimental import pallas as pl
from jax.experimental.pallas import tpu as pltpu

N_META = 16
HEAD = 128
N_HEADS = 8
SEG = N_HEADS * HEAD
N_SEG = 8
SEG_DF_Q, SEG_DF_K = 4, 5
QK_DIM = 64
ROT_HALF = 8
ROPE_THETA = 500000.0
RMS_EPS = 1e-6
SUBLN_EPS = 1e-5
LAMBDA_INIT = 0.8 - 0.6 * math.exp(-0.3 * 0)
SB_SCALE = 1.0 / math.sqrt(HEAD)
SB_DEAD_LOG2 = -130.0
DF_SCALE = 1.0 / math.sqrt(QK_DIM)
LOG2_E = math.log2(math.e)
INPROJ_CHUNK = 256
ONES_ROWS = 16
SB_HEADS_PER_STEP = 4
DF_HEADS_PER_STEP = 4
DF_SCORE_LOOKAHEAD = 2

LANES = 128
V7X_VMEM_LIMIT = 56 * 1024 * 1024

F32 = jnp.float32
BF16 = jnp.bfloat16

_NT = (((1,), (1,)), ((), ()))
_TN = (((0,), (0,)), ((), ()))


def _inproj_kernel(x_ref, nw_ref, w_ref, cw_ref, cos_ref, s1_ref, s2_ref, o_ref, u_ref):
    j = pl.program_id(1)

    @pl.when(j == 0)
    def _():
        x = x_ref[...]
        ms = jnp.mean(x * x, axis=-1, keepdims=True)
        u_ref[...] = (x * lax.rsqrt(ms + RMS_EPS) * nw_ref[...]).astype(BF16)

    is_qk = jnp.logical_or(j == SEG_DF_Q, j == SEG_DF_K)

    def chunks():
        for c in range(o_ref.shape[1] // INPROJ_CHUNK):
            cols = slice(c * INPROJ_CHUNK, (c + 1) * INPROJ_CHUNK)
            yield cols, jnp.dot(u_ref[...], w_ref[:, cols], preferred_element_type=F32)

    @pl.when(jnp.logical_not(is_qk))
    def _():
        for cols, acc in chunks():
            o_ref[:, cols] = (acc * cw_ref[:, cols]).astype(o_ref.dtype)

    @pl.when(is_qk)
    def _():
        first = lax.broadcasted_iota(jnp.int32, (1, LANES), 1) < QK_DIM
        cos, s1, s2 = cos_ref[...], s1_ref[...], s2_ref[...]
        for cols, acc in chunks():
            for c in range(INPROJ_CHUNK // LANES):
                lanes = slice(cols.start + c * LANES, cols.start + (c + 1) * LANES)
                y = acc[:, c * LANES:(c + 1) * LANES]
                y2 = y * y
                sa = jnp.sum(jnp.where(first, y2, 0.0), axis=1, keepdims=True)
                sb = jnp.sum(jnp.where(first, 0.0, y2), axis=1, keepdims=True)
                ms = jnp.where(first, sa, sb) * (1.0 / QK_DIM)
                yn = y * lax.rsqrt(ms + RMS_EPS) * cw_ref[:, lanes]
                out = (yn * cos + pltpu.roll(yn, LANES - ROT_HALF, 1) * s1
                       + pltpu.roll(yn, ROT_HALF, 1) * s2)
                o_ref[:, lanes] = out.astype(o_ref.dtype)


def _inproj(x2d, norm_w, w_bf16, col_w, cos_t, s1_t, s2_t, *, rows_per_seq, tm):
    m, d = x2d.shape
    n = w_bf16.shape[1]
    tn = SEG
    n_pos_tiles = rows_per_seq // tm
    pos_map = lambda i, j: (i % n_pos_tiles, 0)
    return pl.pallas_call(
        _inproj_kernel,
        grid=(m // tm, n // tn),
        in_specs=[
            pl.BlockSpec((tm, d), lambda i, j: (i, 0)),
            pl.BlockSpec((1, d), lambda i, j: (0, 0)),
            pl.BlockSpec((d, tn), lambda i, j: (0, j)),
            pl.BlockSpec((1, tn), lambda i, j: (0, j)),
            pl.BlockSpec((tm, LANES), pos_map),
            pl.BlockSpec((tm, LANES), pos_map),
            pl.BlockSpec((tm, LANES), pos_map),
        ],
        out_specs=pl.BlockSpec((tm, tn), lambda i, j: (i, j)),
        out_shape=jax.ShapeDtypeStruct((m, n), BF16),
        scratch_shapes=[pltpu.VMEM((tm, d), BF16)],
        compiler_params=pltpu.CompilerParams(
            dimension_semantics=("parallel", "arbitrary"),
            vmem_limit_bytes=V7X_VMEM_LIMIT),
        name="inproj",
    )(x2d, norm_w, w_bf16, col_w, cos_t, s1_t, s2_t)


def _silu(g):
    return g * (1.0 / (1.0 + jnp.exp(-g)))


def _head_cols(h):
    return slice(h * HEAD, (h + 1) * HEAD)


def _fill_vt(v_ref, vt_ref, *, blk, heads):
    for h in range(heads):
        for j in range(v_ref.shape[0] // blk):
            vt_ref[h, j, 0:HEAD] = v_ref[j * blk:(j + 1) * blk, _head_cols(h)].T


def _sb_kernel(q_ref, k_ref, v_ref, g_ref, km_ref, vm_ref, tri_ref, o_ref, vt_ref, carry_ref,
               acc_ref, *, blk, heads):
    i = pl.program_id(2)
    hs = range(heads)

    @pl.when(i == 0)
    def _():
        _fill_vt(v_ref, vt_ref, blk=blk, heads=heads)

    def block(keys, weighted_values, tri, valid):
        zs = [lax.dot_general(keys(h), q_ref[:, _head_cols(h)], _NT, preferred_element_type=F32)
              for h in hs]
        log_betas, log_keeps, his, los = [], [], [], []
        for z in zs:
            sp = jnp.log2(1.0 + jnp.exp2(-jnp.abs(z)))
            log_beta = jnp.minimum(z, 0.0) - sp
            log_keep = log_beta - z
            if valid is not None:
                log_keep = jnp.where(valid, log_keep, 0.0)
            hi = log_keep.astype(BF16)
            log_betas.append(log_beta)
            log_keeps.append(log_keep)
            his.append(hi)
            los.append((log_keep - hi.astype(F32)).astype(BF16))
        tails = [jnp.dot(tri, his[h], preferred_element_type=F32)
                 + jnp.dot(tri, los[h], preferred_element_type=F32) for h in hs]
        probs = []
        for h in hs:
            a = jnp.exp2(log_betas[h] + tails[h] + carry_ref[h])
            if valid is not None:
                a = jnp.where(valid, a, 0.0)
            probs.append(a.astype(BF16))
        upd = [weighted_values(h, probs[h]) for h in hs]
        for h in hs:
            acc_ref[h] += upd[h]
            carry_ref[h] += tails[h][0:1, :] + log_keeps[h][0:1, :]

    def real_block(j, valid):
        block(lambda h: k_ref[pl.ds(pl.multiple_of(j * blk, blk), blk), _head_cols(h)],
              lambda h, a: jnp.dot(vt_ref[h, j], a, preferred_element_type=F32),
              tri_ref[...], valid)

    def live():
        return jnp.max(carry_ref[...]) > SB_DEAD_LOG2

    carry_ref[...] = jnp.zeros_like(carry_ref)
    acc_ref[...] = jnp.zeros_like(acc_ref)
    r = lax.broadcasted_iota(jnp.int32, (blk, blk), 0)
    c = lax.broadcasted_iota(jnp.int32, (blk, blk), 1)
    real_block(i, r < c)

    def older(state):
        jj, _ = state
        real_block(i - 1 - jj, None)
        return jj + 1, jnp.logical_and(jj + 1 < i, live())

    lax.while_loop(lambda state: state[1], older,
                   (jnp.int32(0), jnp.logical_and(i > 0, live())))

    @pl.when(live())
    def _():
        block(lambda h: km_ref[:, _head_cols(h)],
              lambda h, a: lax.dot_general(vm_ref[:, _head_cols(h)], a, _TN,
                                           preferred_element_type=F32),
              tri_ref[0:N_META, 0:N_META], None)

    for h in hs:
        g = g_ref[:, _head_cols(h)].astype(F32)
        o_ref[:, _head_cols(h)] = (acc_ref[h].T * _silu(g)).astype(o_ref.dtype)


def _df_kernel(q_ref, k_ref, v_ref, g_ref, km_ref, vm_ref, lq1_ref, lk1_ref, lq2_ref, lk2_ref,
               sw_ref, o_ref, q2_ref, vt_ref, m_ref, acc_ref, *, blk, heads):
    i = pl.program_id(2)
    hs = range(heads)

    @pl.when(i == 0)
    def _():
        _fill_vt(v_ref, vt_ref, blk=blk, heads=heads)
        vt_ref[:, :, HEAD:, :] = jnp.ones_like(vt_ref[:, :, HEAD:, :])

    first = lax.broadcasted_iota(jnp.int32, (1, LANES), 1) < QK_DIM
    for h in hs:
        qf = q_ref[:, _head_cols(h)].astype(F32)
        q2_ref[h, 0:blk, :] = jnp.where(first, qf, 0.0).astype(BF16)
        q2_ref[h, blk:2 * blk, :] = jnp.where(first, 0.0, qf).astype(BF16)
    m_ref[...] = jnp.full_like(m_ref, -jnp.inf)
    acc_ref[...] = jnp.zeros_like(acc_ref)
    ones_meta = jnp.ones((N_META, ONES_ROWS), BF16)

    def block(j0, n_blocks, diag_row, with_meta):
        off = N_META if with_meta else 0
        size = off + n_blocks * blk
        rows = pl.ds(pl.multiple_of(j0 * blk, blk), n_blocks * blk)

        def keys(h):
            k = k_ref[rows, _head_cols(h)]
            return jnp.concatenate([km_ref[:, _head_cols(h)], k], axis=0) if with_meta else k

        if diag_row is not None:
            r = lax.broadcasted_iota(jnp.int32, (size, 2 * blk), 0)
            c = lax.broadcasted_iota(jnp.int32, (size, 2 * blk), 1)
            valid = r - (off + diag_row) <= jnp.where(c >= blk, c - blk, c)
        scores, alphas, probs = {}, {}, {}

        def score_matmul(h):
            scores[h] = lax.dot_general(keys(h), q2_ref[h], _NT, preferred_element_type=F32)

        def softmax(h):
            s = scores[h] if diag_row is None else jnp.where(valid, scores[h], -jnp.inf)
            m_new = jnp.maximum(m_ref[h], jnp.max(s, axis=0, keepdims=True))
            alphas[h] = jnp.exp2(m_ref[h] - m_new)
            m_ref[h] = m_new
            probs[h] = jnp.exp2(s - m_new).astype(BF16)

        def value_matmul(h):
            u = sum(jnp.dot(vt_ref[h, j0 + n], probs[h][off + n * blk:off + (n + 1) * blk],
                            preferred_element_type=F32) for n in range(n_blocks))
            if with_meta:
                vm1 = jnp.concatenate([vm_ref[:, _head_cols(h)], ones_meta], axis=1)
                u = u + lax.dot_general(vm1, probs[h][0:off], _TN, preferred_element_type=F32)
            acc_ref[h] = alphas[h] * acc_ref[h] + u

        for h in range(min(DF_SCORE_LOOKAHEAD, heads)):
            score_matmul(h)
        for h in hs:
            softmax(h)
            if h + DF_SCORE_LOOKAHEAD < heads:
                score_matmul(h + DF_SCORE_LOOKAHEAD)
            if h >= 1:
                value_matmul(h - 1)
        value_matmul(heads - 1)

    @pl.loop(0, i // 2)
    def _(t):
        block(2 * t, 2, None, False)

    @pl.when(i % 2 == 1)
    def _():
        block(i - 1, 2, blk, True)

    @pl.when(i % 2 == 0)
    def _():
        block(i, 1, 0, True)

    lam = (jnp.exp(jnp.sum(lq1_ref[...] * lk1_ref[...], axis=1, keepdims=True))
           - jnp.exp(jnp.sum(lq2_ref[...] * lk2_ref[...], axis=1, keepdims=True))
           + LAMBDA_INIT)
    for h in hs:
        o = acc_ref[h, 0:HEAD] / acc_ref[h, HEAD:HEAD + 1]
        o = o[:, :blk] - lam * o[:, blk:]
        ms = jnp.mean(o * o, axis=0, keepdims=True)
        y = (o * lax.rsqrt(ms + SUBLN_EPS)).T * (sw_ref[...] * (1.0 - LAMBDA_INIT))
        g = g_ref[:, _head_cols(h)].astype(F32)
        o_ref[:, _head_cols(h)] = (y * _silu(g)).astype(o_ref.dtype)


def _attention(kernel_fn, proj, proj_meta, extra, extra_specs, scratch, *, seg0, blk, heads,
               name):
    b, s, _ = proj.shape
    groups = N_HEADS // heads
    width = heads * HEAD

    def col(seg):
        return lambda bi, h, i, seg=seg: (bi, i, seg * groups + h)

    def full(seg):
        return lambda bi, h, i, seg=seg: (bi, 0, seg * groups + h)

    def meta(seg):
        return lambda bi, h, i, seg=seg: (0, seg * groups + h)

    return pl.pallas_call(
        functools.partial(kernel_fn, blk=blk, heads=heads),
        grid=(b, groups, s // blk),
        in_specs=[
            pl.BlockSpec((None, blk, width), col(seg0)),
            pl.BlockSpec((None, s, width), full(seg0 + 1)),
            pl.BlockSpec((None, s, width), full(seg0 + 2)),
            pl.BlockSpec((None, blk, width), col(seg0 + 3)),
            pl.BlockSpec((N_META, width), meta(seg0 + 1)),
            pl.BlockSpec((N_META, width), meta(seg0 + 2)),
        ] + extra_specs,
        out_specs=pl.BlockSpec((None, blk, width), lambda bi, h, i: (bi, i, h)),
        out_shape=jax.ShapeDtypeStruct((b, s, SEG), BF16),
        scratch_shapes=scratch,
        compiler_params=pltpu.CompilerParams(
            dimension_semantics=("parallel", "parallel", "arbitrary"),
            vmem_limit_bytes=V7X_VMEM_LIMIT),
        name=name,
    )(proj, proj, proj, proj, proj_meta, proj_meta, *extra)


def _outproj_kernel(sb_ref, df_ref, wsb_ref, wdf_ref, x_ref, o_ref):
    y = (jnp.dot(sb_ref[...], wsb_ref[...], preferred_element_type=F32)
         + jnp.dot(df_ref[...], wdf_ref[...], preferred_element_type=F32))
    o_ref[...] = x_ref[...] + y


def _outproj(sb_mixed, df_mixed, w_out_bf16, x2d, *, tm):
    m, d = x2d.shape
    const = lambda i: (0, 0)
    return pl.pallas_call(
        _outproj_kernel,
        grid=(m // tm,),
        in_specs=[
            pl.BlockSpec((tm, SEG), lambda i: (i, 0)),
            pl.BlockSpec((tm, SEG), lambda i: (i, 0)),
            pl.BlockSpec((SEG, d), lambda i: (0, 0)),
            pl.BlockSpec((SEG, d), lambda i: (1, 0)),
            pl.BlockSpec((tm, d), lambda i: (i, 0)),
        ],
        out_specs=pl.BlockSpec((tm, d), lambda i: (i, 0)),
        out_shape=jax.ShapeDtypeStruct((m, d), x2d.dtype),
        compiler_params=pltpu.CompilerParams(
            dimension_semantics=("parallel",),
            vmem_limit_bytes=V7X_VMEM_LIMIT),
        name="outproj",
    )(sb_mixed, df_mixed, w_out_bf16, w_out_bf16, x2d)


def _rope_tables(t_all):
    pos = jnp.arange(t_all, dtype=F32)
    inv_freq = ROPE_THETA ** (-jnp.arange(0, 2 * ROT_HALF, 2, dtype=F32) / (2 * ROT_HALF))
    ang = pos[:, None] * inv_freq[None, :]
    cos, sin = jnp.cos(ang), jnp.sin(ang)
    lane = jnp.arange(LANES) % QK_DIM
    pair = lane % ROT_HALF
    cos_t = jnp.where(lane < 2 * ROT_HALF, cos[:, pair], 1.0)
    s1_t = jnp.where(lane < ROT_HALF, -sin[:, pair], 0.0)
    s2_t = jnp.where((lane >= ROT_HALF) & (lane < 2 * ROT_HALF), sin[:, pair], 0.0)
    return cos_t, s1_t, s2_t


def kernel(x, meta, norm_w, w_in, q_norm_w, k_norm_w, lambda_q1, lambda_k1, lambda_q2,
           lambda_k2, subln_w, w_out):
    b, s, d = x.shape
    assert norm_w.shape[0] == 1, "single-layer problem"
    blk = 256
    tm = min(1024, s)
    tm_out = min(512, s)
    assert s % tm == 0 and s % (2 * blk) == 0 and s % tm_out == 0

    w_in_b = w_in[0].astype(BF16)
    w_out_b = w_out[0].astype(BF16)
    col_w = jnp.ones((1, N_SEG * SEG), F32).at[0, 0:SEG].set(SB_SCALE * LOG2_E)
    col_w = col_w.at[0, SEG_DF_Q * SEG:(SEG_DF_Q + 1) * SEG].set(
        jnp.tile(q_norm_w[0] * (DF_SCALE * LOG2_E), SEG // QK_DIM))
    col_w = col_w.at[0, SEG_DF_K * SEG:(SEG_DF_K + 1) * SEG].set(
        jnp.tile(k_norm_w[0], SEG // QK_DIM))
    cos_t, s1_t, s2_t = _rope_tables(N_META + s)

    x2d = x.reshape(b * s, d)
    proj = _inproj(x2d, norm_w, w_in_b, col_w, cos_t[N_META:], s1_t[N_META:], s2_t[N_META:],
                   rows_per_seq=s, tm=tm).reshape(b, s, N_SEG * SEG)
    proj_meta = _inproj(meta, norm_w, w_in_b, col_w, cos_t[:N_META], s1_t[:N_META],
                        s2_t[:N_META], rows_per_seq=N_META, tm=N_META)

    tri = (jnp.arange(blk)[None, :] > jnp.arange(blk)[:, None]).astype(BF16)
    const2 = lambda bi, h, i: (0, 0)
    sb_mixed = _attention(_sb_kernel, proj, proj_meta, [tri],
                          [pl.BlockSpec((blk, blk), const2)],
                          [pltpu.VMEM((SB_HEADS_PER_STEP, s // blk, HEAD, blk), BF16),
                           pltpu.VMEM((SB_HEADS_PER_STEP, 1, blk), F32),
                           pltpu.VMEM((SB_HEADS_PER_STEP, HEAD, blk), F32)],
                          seg0=0, blk=blk, heads=SB_HEADS_PER_STEP, name="sb_attn")
    vec = pl.BlockSpec((1, QK_DIM), const2)
    df_mixed = _attention(_df_kernel, proj, proj_meta,
                          [lambda_q1, lambda_k1, lambda_q2, lambda_k2, subln_w],
                          [vec, vec, vec, vec, pl.BlockSpec((1, HEAD), const2)],
                          [pltpu.VMEM((DF_HEADS_PER_STEP, 2 * blk, HEAD), BF16),
                           pltpu.VMEM((DF_HEADS_PER_STEP, s // blk, HEAD + ONES_ROWS, blk), BF16),
                           pltpu.VMEM((DF_HEADS_PER_STEP, 1, 2 * blk), F32),
                           pltpu.VMEM((DF_HEADS_PER_STEP, HEAD + ONES_ROWS, 2 * blk), F32)],
                          seg0=4, blk=blk, heads=DF_HEADS_PER_STEP, name="df_attn")
    out = _outproj(sb_mixed.reshape(b * s, SEG), df_mixed.reshape(b * s, SEG), w_out_b, x2d,
                   tm=tm_out)
    return out.reshape(b, s, d)
```

```python
import functools
import math

import jax
import jax.numpy as jnp
from jax import lax
from jax.experimental import pallas as pl
from jax.experimental.pallas import tpu as pltpu

N_META = 16
HEAD = 128
N_HEADS = 8
SEG = N_HEADS * HEAD
N_SEG = 8
SEG_DF_Q, SEG_DF_K = 4, 5
QK_DIM = 64
ROT_HALF = 8
ROPE_THETA = 500000.0
RMS_EPS = 1e-6
SUBLN_EPS = 1e-5
LAMBDA_INIT = 0.8 - 0.6 * math.exp(-0.3 * 0)
SB_SCALE = 1.0 / math.sqrt(HEAD)
SB_DEAD_LOG2 = -130.0
DF_SCALE = 1.0 / math.sqrt(QK_DIM)
LOG2_E = math.log2(math.e)
INPROJ_CHUNK = 256
ONES_ROWS = 16
SB_HEADS_PER_STEP = 4
DF_HEADS_PER_STEP = 4

LANES = 128
V7X_VMEM_LIMIT = 56 * 1024 * 1024

F32 = jnp.float32
BF16 = jnp.bfloat16

_NT = (((1,), (1,)), ((), ()))
_TN = (((0,), (0,)), ((), ()))


def _inproj_kernel(x_ref, nw_ref, w_ref, cw_ref, qkw_ref, cos_ref, s1_ref, s2_ref,
                   oqk_ref, opl_ref, u_ref, raw_ref):
    j = pl.program_id(1)

    @pl.when(j == 0)
    def _():
        x = x_ref[...]
        ms = jnp.mean(x * x, axis=-1, keepdims=True)
        u_ref[...] = (x * lax.rsqrt(ms + RMS_EPS) * nw_ref[...]).astype(BF16)

    def chunks():
        for c in range(w_ref.shape[1] // INPROJ_CHUNK):
            cols = slice(c * INPROJ_CHUNK, (c + 1) * INPROJ_CHUNK)
            yield cols, jnp.dot(u_ref[...], w_ref[:, cols], preferred_element_type=F32)

    def park(slot):
        for cols, acc in chunks():
            raw_ref[slot, :, cols] = acc

    def plain():
        for cols, acc in chunks():
            opl_ref[:, cols] = (acc * cw_ref[:, cols]).astype(opl_ref.dtype)

    def qk_epilogue(slot):
        first = lax.broadcasted_iota(jnp.int32, (1, LANES), 1) < QK_DIM
        cos, s1, s2 = cos_ref[...], s1_ref[...], s2_ref[...]
        for c in range(raw_ref.shape[2] // LANES):
            lanes = slice(c * LANES, (c + 1) * LANES)
            y = raw_ref[slot, :, lanes]
            y2 = y * y
            sa = jnp.sum(jnp.where(first, y2, 0.0), axis=1, keepdims=True)
            sb = jnp.sum(jnp.where(first, 0.0, y2), axis=1, keepdims=True)
            ms = jnp.where(first, sa, sb) * (1.0 / QK_DIM)
            yn = y * lax.rsqrt(ms + RMS_EPS) * qkw_ref[slot:slot + 1, lanes]
            out = (yn * cos + pltpu.roll(yn, LANES - ROT_HALF, 1) * s1
                   + pltpu.roll(yn, ROT_HALF, 1) * s2)
            oqk_ref[:, lanes] = out.astype(oqk_ref.dtype)

    @pl.when(j == 0)
    def _():
        park(0)

    @pl.when(j == 1)
    def _():
        park(1)
        qk_epilogue(0)

    @pl.when(j == 2)
    def _():
        plain()
        qk_epilogue(1)

    @pl.when(j > 2)
    def _():
        plain()


def _inproj_segment(j):
    return jnp.where(j == 0, SEG_DF_Q, jnp.where(j == 1, SEG_DF_K,
                                                 jnp.where(j < SEG_DF_Q + 2, j - 2, j)))


def _inproj(x2d, norm_w, w_bf16, plain_w, qk_w, cos_t, s1_t, s2_t, *, rows_per_seq, tm):
    m, d = x2d.shape
    tn = SEG
    n_pos_tiles = rows_per_seq // tm
    pos_map = lambda i, j: (i % n_pos_tiles, 0)
    plain_map = lambda i, j: (i, jnp.maximum(j - 2, 0))
    return pl.pallas_call(
        _inproj_kernel,
        grid=(m // tm, N_SEG),
        in_specs=[
            pl.BlockSpec((tm, d), lambda i, j: (i, 0)),
            pl.BlockSpec((1, d), lambda i, j: (0, 0)),
            pl.BlockSpec((d, tn), lambda i, j: (0, _inproj_segment(j))),
            pl.BlockSpec((1, tn), lambda i, j: (0, jnp.maximum(j - 2, 0))),
            pl.BlockSpec((2, tn), lambda i, j: (0, 0)),
            pl.BlockSpec((tm, LANES), pos_map),
            pl.BlockSpec((tm, LANES), pos_map),
            pl.BlockSpec((tm, LANES), pos_map),
        ],
        out_specs=[pl.BlockSpec((tm, tn), lambda i, j: (i, jnp.where(j < 2, 0, 1))),
                   pl.BlockSpec((tm, tn), plain_map)],
        out_shape=[jax.ShapeDtypeStruct((m, 2 * SEG), BF16),
                   jax.ShapeDtypeStruct((m, (N_SEG - 2) * SEG), BF16)],
        scratch_shapes=[pltpu.VMEM((tm, d), BF16), pltpu.VMEM((2, tm, tn), F32)],
        compiler_params=pltpu.CompilerParams(
            dimension_semantics=("parallel", "arbitrary"),
            vmem_limit_bytes=V7X_VMEM_LIMIT),
        name="inproj",
    )(x2d, norm_w, w_bf16, plain_w, qk_w, cos_t, s1_t, s2_t)


def _silu(g):
    return g * (1.0 / (1.0 + jnp.exp(-g)))


def _head_cols(h):
    return slice(h * HEAD, (h + 1) * HEAD)


def _fill_vt(v_ref, vt_ref, *, blk, heads):
    for h in range(heads):
        for j in range(v_ref.shape[0] // blk):
            vt_ref[h, j, 0:HEAD] = v_ref[j * blk:(j + 1) * blk, _head_cols(h)].T


def _sb_kernel(q_ref, k_ref, v_ref, g_ref, km_ref, vm_ref, tri_ref, o_ref, vt_ref, carry_ref,
               acc_ref, *, blk, heads):
    i = pl.program_id(2)
    hs = range(heads)

    @pl.when(i == 0)
    def _():
        _fill_vt(v_ref, vt_ref, blk=blk, heads=heads)

    def blocks(parts):
        units = [(part, h) for part in parts for h in hs]
        zs = [lax.dot_general(keys(h), q_ref[:, _head_cols(h)], _NT, preferred_element_type=F32)
              for (keys, _, _, _), h in units]
        log_betas, log_keeps, his, los = [], [], [], []
        for z, ((_, _, _, valid), _) in zip(zs, units):
            sp = jnp.log2(1.0 + jnp.exp2(-jnp.abs(z)))
            log_beta = jnp.minimum(z, 0.0) - sp
            log_keep = log_beta - z
            if valid is not None:
                log_keep = jnp.where(valid, log_keep, 0.0)
            hi = log_keep.astype(BF16)
            log_betas.append(log_beta)
            log_keeps.append(log_keep)
            his.append(hi)
            los.append((log_keep - hi.astype(F32)).astype(BF16))
        tails = [jnp.dot(tri, his[u], preferred_element_type=F32)
                 + jnp.dot(tri, los[u], preferred_element_type=F32)
                 for u, ((_, _, tri, _), _) in enumerate(units)]
        carry = {h: carry_ref[h] for h in hs}
        probs = []
        for u, ((_, _, _, valid), h) in enumerate(units):
            a = jnp.exp2(log_betas[u] + tails[u] + carry[h])
            if valid is not None:
                a = jnp.where(valid, a, 0.0)
            probs.append(a.astype(BF16))
            carry[h] = carry[h] + tails[u][0:1, :] + log_keeps[u][0:1, :]
        upd = [weighted_values(h, probs[u]) for u, ((_, weighted_values, _, _), h)
               in enumerate(units)]
        for u, (_, h) in enumerate(units):
            acc_ref[h] += upd[u]
        for h in hs:
            carry_ref[h] = carry[h]

    def real_part(j, valid):
        return (lambda h: k_ref[pl.ds(pl.multiple_of(j * blk, blk), blk), _head_cols(h)],
                lambda h, a: jnp.dot(vt_ref[h, j], a, preferred_element_type=F32),
                tri_ref[...], valid)

    def live():
        return jnp.max(carry_ref[...]) > SB_DEAD_LOG2

    carry_ref[...] = jnp.zeros_like(carry_ref)
    acc_ref[...] = jnp.zeros_like(acc_ref)
    r = lax.broadcasted_iota(jnp.int32, (blk, blk), 0)
    c = lax.broadcasted_iota(jnp.int32, (blk, blk), 1)
    @pl.when(i == 0)
    def _():
        blocks([real_part(0, r < c)])

    @pl.when(i > 0)
    def _():
        blocks([real_part(i, r < c), real_part(i - 1, None)])

    def older(state):
        jj, _ = state
        blocks([real_part(i - 1 - jj, None)])
        return jj + 1, jnp.logical_and(jj + 1 < i, live())

    lax.while_loop(lambda state: state[1], older,
                   (jnp.int32(1), jnp.logical_and(i > 1, live())))

    @pl.when(live())
    def _():
        blocks([(lambda h: km_ref[:, _head_cols(h)],
                 lambda h, a: lax.dot_general(vm_ref[:, _head_cols(h)], a, _TN,
                                              preferred_element_type=F32),
                 tri_ref[0:N_META, 0:N_META], None)])

    for h in hs:
        g = g_ref[:, _head_cols(h)].astype(F32)
        o_ref[:, _head_cols(h)] = (acc_ref[h].T * _silu(g)).astype(o_ref.dtype)


def _df_kernel(q_ref, k_ref, v_ref, g_ref, km_ref, vm_ref, lq1_ref, lk1_ref, lq2_ref, lk2_ref,
               sw_ref, o_ref, q2_ref, vt_ref, m_ref, acc_ref, *, blk, heads):
    i = pl.program_id(2)
    hs = range(heads)

    @pl.when(i == 0)
    def _():
        _fill_vt(v_ref, vt_ref, blk=blk, heads=heads)
        vt_ref[:, :, HEAD:, :] = jnp.ones_like(vt_ref[:, :, HEAD:, :])

    first = lax.broadcasted_iota(jnp.int32, (1, LANES), 1) < QK_DIM
    for h in hs:
        qf = q_ref[:, _head_cols(h)].astype(F32)
        q2_ref[h, 0:blk, :] = jnp.where(first, qf, 0.0).astype(BF16)
        q2_ref[h, blk:2 * blk, :] = jnp.where(first, 0.0, qf).astype(BF16)
    m_ref[...] = jnp.full_like(m_ref, -jnp.inf)
    acc_ref[...] = jnp.zeros_like(acc_ref)
    ones_meta = jnp.ones((N_META, ONES_ROWS), BF16)

    def block(j0, n_blocks, diag_row, with_meta):
        off = N_META if with_meta else 0
        size = off + n_blocks * blk
        rows = pl.ds(pl.multiple_of(j0 * blk, blk), n_blocks * blk)

        def keys(h):
            k = k_ref[rows, _head_cols(h)]
            return jnp.concatenate([km_ref[:, _head_cols(h)], k], axis=0) if with_meta else k

        if diag_row is not None:
            r = lax.broadcasted_iota(jnp.int32, (size, 2 * blk), 0)
            c = lax.broadcasted_iota(jnp.int32, (size, 2 * blk), 1)
            valid = r - (off + diag_row) <= jnp.where(c >= blk, c - blk, c)
        scores, alphas, probs = {}, {}, {}

        def score_matmul(h):
            scores[h] = lax.dot_general(keys(h), q2_ref[h], _NT, preferred_element_type=F32)

        def softmax(h):
            s = scores[h] if diag_row is None else jnp.where(valid, scores[h], -jnp.inf)
            m_new = jnp.maximum(m_ref[h], jnp.max(s, axis=0, keepdims=True))
            alphas[h] = jnp.exp2(m_ref[h] - m_new)
            m_ref[h] = m_new
            probs[h] = jnp.exp2(s - m_new).astype(BF16)

        def value_matmul(h):
            u = sum(jnp.dot(vt_ref[h, j0 + n], probs[h][off + n * blk:off + (n + 1) * blk],
                            preferred_element_type=F32) for n in range(n_blocks))
            if with_meta:
                vm1 = jnp.concatenate([vm_ref[:, _head_cols(h)], ones_meta], axis=1)
                u = u + lax.dot_general(vm1, probs[h][0:off], _TN, preferred_element_type=F32)
            acc_ref[h] = alphas[h] * acc_ref[h] + u

        for stage in (score_matmul, softmax, value_matmul):
            for h in hs:
                stage(h)

    @pl.loop(0, i // 2)
    def _(t):
        block(2 * t, 2, None, False)

    @pl.when(i % 2 == 1)
    def _():
        block(i - 1, 2, blk, True)

    @pl.when(i % 2 == 0)
    def _():
        block(i, 1, 0, True)

    lam = (jnp.exp(jnp.sum(lq1_ref[...] * lk1_ref[...], axis=1, keepdims=True))
           - jnp.exp(jnp.sum(lq2_ref[...] * lk2_ref[...], axis=1, keepdims=True))
           + LAMBDA_INIT)
    for h in hs:
        o = acc_ref[h, 0:HEAD] / acc_ref[h, HEAD:HEAD + 1]
        o = o[:, :blk] - lam * o[:, blk:]
        ms = jnp.mean(o * o, axis=0, keepdims=True)
        y = (o * lax.rsqrt(ms + SUBLN_EPS)).T * (sw_ref[...] * (1.0 - LAMBDA_INIT))
        g = g_ref[:, _head_cols(h)].astype(F32)
        o_ref[:, _head_cols(h)] = (y * _silu(g)).astype(o_ref.dtype)


def _attention(kernel_fn, q, k, v, g, meta_k, meta_v, extra, extra_specs, scratch, *, blk,
               heads, name):
    b, s, _ = q[0].shape
    groups = N_HEADS // heads
    width = heads * HEAD

    def col(seg):
        return lambda bi, h, i, seg=seg: (bi, i, seg * groups + h)

    def full(seg):
        return lambda bi, h, i, seg=seg: (bi, 0, seg * groups + h)

    def meta(seg):
        return lambda bi, h, i, seg=seg: (0, seg * groups + h)

    return pl.pallas_call(
        functools.partial(kernel_fn, blk=blk, heads=heads),
        grid=(b, groups, s // blk),
        in_specs=[
            pl.BlockSpec((None, blk, width), col(q[1])),
            pl.BlockSpec((None, s, width), full(k[1])),
            pl.BlockSpec((None, s, width), full(v[1])),
            pl.BlockSpec((None, blk, width), col(g[1])),
            pl.BlockSpec((N_META, width), meta(meta_k[1])),
            pl.BlockSpec((N_META, width), meta(meta_v[1])),
        ] + extra_specs,
        out_specs=pl.BlockSpec((None, blk, width), lambda bi, h, i: (bi, i, h)),
        out_shape=jax.ShapeDtypeStruct((b, s, SEG), BF16),
        scratch_shapes=scratch,
        compiler_params=pltpu.CompilerParams(
            dimension_semantics=("parallel", "parallel", "arbitrary"),
            vmem_limit_bytes=V7X_VMEM_LIMIT),
        name=name,
    )(q[0], k[0], v[0], g[0], meta_k[0], meta_v[0], *extra)


def _outproj_kernel(sb_ref, df_ref, wsb_ref, wdf_ref, x_ref, o_ref):
    y = (jnp.dot(sb_ref[...], wsb_ref[...], preferred_element_type=F32)
         + jnp.dot(df_ref[...], wdf_ref[...], preferred_element_type=F32))
    o_ref[...] = x_ref[...] + y


def _outproj(sb_mixed, df_mixed, w_out_bf16, x2d, *, tm):
    m, d = x2d.shape
    const = lambda i: (0, 0)
    return pl.pallas_call(
        _outproj_kernel,
        grid=(m // tm,),
        in_specs=[
            pl.BlockSpec((tm, SEG), lambda i: (i, 0)),
            pl.BlockSpec((tm, SEG), lambda i: (i, 0)),
            pl.BlockSpec((SEG, d), lambda i: (0, 0)),
            pl.BlockSpec((SEG, d), lambda i: (1, 0)),
            pl.BlockSpec((tm, d), lambda i: (i, 0)),
        ],
        out_specs=pl.BlockSpec((tm, d), lambda i: (i, 0)),
        out_shape=jax.ShapeDtypeStruct((m, d), x2d.dtype),
        compiler_params=pltpu.CompilerParams(
            dimension_semantics=("parallel",),
            vmem_limit_bytes=V7X_VMEM_LIMIT),
        name="outproj",
    )(sb_mixed, df_mixed, w_out_bf16, w_out_bf16, x2d)


def _rope_tables(t_all):
    pos = jnp.arange(t_all, dtype=F32)
    inv_freq = ROPE_THETA ** (-jnp.arange(0, 2 * ROT_HALF, 2, dtype=F32) / (2 * ROT_HALF))
    ang = pos[:, None] * inv_freq[None, :]
    cos, sin = jnp.cos(ang), jnp.sin(ang)
    lane = jnp.arange(LANES) % QK_DIM
    pair = lane % ROT_HALF
    cos_t = jnp.where(lane < 2 * ROT_HALF, cos[:, pair], 1.0)
    s1_t = jnp.where(lane < ROT_HALF, -sin[:, pair], 0.0)
    s2_t = jnp.where((lane >= ROT_HALF) & (lane < 2 * ROT_HALF), sin[:, pair], 0.0)
    return cos_t, s1_t, s2_t


def kernel(x, meta, norm_w, w_in, q_norm_w, k_norm_w, lambda_q1, lambda_k1, lambda_q2,
           lambda_k2, subln_w, w_out):
    b, s, d = x.shape
    assert norm_w.shape[0] == 1, "single-layer problem"
    blk = 256
    tm = min(1024, s)
    tm_out = min(512, s)
    assert s % tm == 0 and s % (2 * blk) == 0 and s % tm_out == 0

    w_in_b = w_in[0].astype(BF16)
    w_out_b = w_out[0].astype(BF16)
    plain_w = jnp.ones((1, (N_SEG - 2) * SEG), F32).at[0, 0:SEG].set(SB_SCALE * LOG2_E)
    qk_w = jnp.stack([jnp.tile(q_norm_w[0] * (DF_SCALE * LOG2_E), SEG // QK_DIM),
                      jnp.tile(k_norm_w[0], SEG // QK_DIM)])
    cos_t, s1_t, s2_t = _rope_tables(N_META + s)

    x2d = x.reshape(b * s, d)
    qk, plain = _inproj(x2d, norm_w, w_in_b, plain_w, qk_w, cos_t[N_META:], s1_t[N_META:],
                        s2_t[N_META:], rows_per_seq=s, tm=tm)
    qk, plain = qk.reshape(b, s, -1), plain.reshape(b, s, -1)
    meta_qk, meta_plain = _inproj(meta, norm_w, w_in_b, plain_w, qk_w, cos_t[:N_META],
                                  s1_t[:N_META], s2_t[:N_META], rows_per_seq=N_META,
                                  tm=N_META)

    tri = (jnp.arange(blk)[None, :] > jnp.arange(blk)[:, None]).astype(BF16)
    const2 = lambda bi, h, i: (0, 0)
    sb_mixed = _attention(_sb_kernel, (plain, 0), (plain, 1), (plain, 2), (plain, 3),
                          (meta_plain, 1), (meta_plain, 2), [tri],
                          [pl.BlockSpec((blk, blk), const2)],
                          [pltpu.VMEM((SB_HEADS_PER_STEP, s // blk, HEAD, blk), BF16),
                           pltpu.VMEM((SB_HEADS_PER_STEP, 1, blk), F32),
                           pltpu.VMEM((SB_HEADS_PER_STEP, HEAD, blk), F32)],
                          blk=blk, heads=SB_HEADS_PER_STEP, name="sb_attn")
    vec = pl.BlockSpec((1, QK_DIM), const2)
    df_mixed = _attention(_df_kernel, (qk, 0), (qk, 1), (plain, 4), (plain, 5),
                          (meta_qk, 1), (meta_plain, 4),
                          [lambda_q1, lambda_k1, lambda_q2, lambda_k2, subln_w],
                          [vec, vec, vec, vec, pl.BlockSpec((1, HEAD), const2)],
                          [pltpu.VMEM((DF_HEADS_PER_STEP, 2 * blk, HEAD), BF16),
                           pltpu.VMEM((DF_HEADS_PER_STEP, s // blk, HEAD + ONES_ROWS, blk), BF16),
                           pltpu.VMEM((DF_HEADS_PER_STEP, 1, 2 * blk), F32),
                           pltpu.VMEM((DF_HEADS_PER_STEP, HEAD + ONES_ROWS, 2 * blk), F32)],
                          blk=blk, heads=DF_HEADS_PER_STEP, name="df_attn")
    out = _outproj(sb_mixed.reshape(b * s, SEG), df_mixed.reshape(b * s, SEG), w_out_b, x2d,
                   tm=tm_out)
    return out.reshape(b, s, d)
```

```python
import functools
import math

import jax
import jax.numpy as jnp
from jax import lax
from jax.experimental import pallas as pl
from jax.experimental.pallas import tpu as pltpu

N_META = 16
HEAD = 128
N_HEADS = 8
SEG = N_HEADS * HEAD
N_SEG = 8
SEG_DF_Q, SEG_DF_K = 4, 5
QK_DIM = 64
ROT_HALF = 8
ROPE_THETA = 500000.0
RMS_EPS = 1e-6
SUBLN_EPS = 1e-5
LAMBDA_INIT = 0.8 - 0.6 * math.exp(-0.3 * 0)
SB_SCALE = 1.0 / math.sqrt(HEAD)
SB_DEAD_LOG2 = -130.0
DF_SCALE = 1.0 / math.sqrt(QK_DIM)
LOG2_E = math.log2(math.e)
ATTN_BLOCK = 256
INPROJ_ROWS = 1024
OUTPROJ_ROWS = 512
INPROJ_CHUNK = 256
ONES_ROWS = 16
SB_HEADS_PER_STEP = 8
DF_HEADS_PER_STEP = 8

LANES = 128
V7X_VMEM_LIMIT = 56 * 1024 * 1024

F32 = jnp.float32
BF16 = jnp.bfloat16

_NT = (((1,), (1,)), ((), ()))
_TN = (((0,), (0,)), ((), ()))


def _inproj_kernel(x_ref, nw_ref, w_ref, cw_ref, qkw_ref, cos_ref, s1_ref, s2_ref,
                   oqk_ref, opl_ref, u_ref, raw_ref):
    j = pl.program_id(1)

    @pl.when(j == 0)
    def _():
        x = x_ref[...]
        ms = jnp.mean(x * x, axis=-1, keepdims=True)
        u_ref[...] = (x * lax.rsqrt(ms + RMS_EPS) * nw_ref[...]).astype(BF16)

    def chunks():
        for c in range(w_ref.shape[1] // INPROJ_CHUNK):
            cols = slice(c * INPROJ_CHUNK, (c + 1) * INPROJ_CHUNK)
            yield cols, jnp.dot(u_ref[...], w_ref[:, cols], preferred_element_type=F32)

    def park(slot):
        for cols, acc in chunks():
            raw_ref[slot, :, cols] = acc

    def plain():
        for cols, acc in chunks():
            opl_ref[:, cols] = (acc * cw_ref[:, cols]).astype(opl_ref.dtype)

    def qk_epilogue(slot):
        first = lax.broadcasted_iota(jnp.int32, (1, LANES), 1) < QK_DIM
        cos, s1, s2 = cos_ref[...], s1_ref[...], s2_ref[...]
        for c in range(raw_ref.shape[2] // LANES):
            lanes = slice(c * LANES, (c + 1) * LANES)
            y = raw_ref[slot, :, lanes]
            y2 = y * y
            sa = jnp.sum(jnp.where(first, y2, 0.0), axis=1, keepdims=True)
            sb = jnp.sum(jnp.where(first, 0.0, y2), axis=1, keepdims=True)
            ms = jnp.where(first, sa, sb) * (1.0 / QK_DIM)
            yn = y * lax.rsqrt(ms + RMS_EPS) * qkw_ref[slot:slot + 1, lanes]
            out = (yn * cos + pltpu.roll(yn, LANES - ROT_HALF, 1) * s1
                   + pltpu.roll(yn, ROT_HALF, 1) * s2)
            oqk_ref[:, lanes] = out.astype(oqk_ref.dtype)

    @pl.when(j == 0)
    def _():
        park(0)

    @pl.when(j == 1)
    def _():
        park(1)
        qk_epilogue(0)

    @pl.when(j == 2)
    def _():
        plain()
        qk_epilogue(1)

    @pl.when(j > 2)
    def _():
        plain()


def _inproj_segment(j):
    return jnp.where(j == 0, SEG_DF_Q, jnp.where(j == 1, SEG_DF_K,
                                                 jnp.where(j < SEG_DF_Q + 2, j - 2, j)))


def _inproj(x2d, norm_w, w_bf16, plain_w, qk_w, cos_t, s1_t, s2_t, *, rows_per_seq, tm):
    m, d = x2d.shape
    tn = SEG
    n_pos_tiles = rows_per_seq // tm
    pos_map = lambda i, j: (i % n_pos_tiles, 0)
    plain_map = lambda i, j: (i, jnp.maximum(j - 2, 0))
    return pl.pallas_call(
        _inproj_kernel,
        grid=(m // tm, N_SEG),
        in_specs=[
            pl.BlockSpec((tm, d), lambda i, j: (i, 0)),
            pl.BlockSpec((1, d), lambda i, j: (0, 0)),
            pl.BlockSpec((d, tn), lambda i, j: (0, _inproj_segment(j))),
            pl.BlockSpec((1, tn), lambda i, j: (0, jnp.maximum(j - 2, 0))),
            pl.BlockSpec((2, tn), lambda i, j: (0, 0)),
            pl.BlockSpec((tm, LANES), pos_map),
            pl.BlockSpec((tm, LANES), pos_map),
            pl.BlockSpec((tm, LANES), pos_map),
        ],
        out_specs=[pl.BlockSpec((tm, tn), lambda i, j: (i, jnp.where(j < 2, 0, 1))),
                   pl.BlockSpec((tm, tn), plain_map)],
        out_shape=[jax.ShapeDtypeStruct((m, 2 * SEG), BF16),
                   jax.ShapeDtypeStruct((m, (N_SEG - 2) * SEG), BF16)],
        scratch_shapes=[pltpu.VMEM((tm, d), BF16), pltpu.VMEM((2, tm, tn), F32)],
        compiler_params=pltpu.CompilerParams(
            dimension_semantics=("parallel", "arbitrary"),
            vmem_limit_bytes=V7X_VMEM_LIMIT),
        name="inproj",
    )(x2d, norm_w, w_bf16, plain_w, qk_w, cos_t, s1_t, s2_t)


def _silu(g):
    return g * (1.0 / (1.0 + jnp.exp(-g)))


def _head_cols(h):
    return slice(h * HEAD, (h + 1) * HEAD)


def _fill_vt(v_ref, vt_ref, *, blk, heads):
    for h in range(heads):
        for j in range(v_ref.shape[0] // blk):
            vt_ref[h, j, 0:HEAD] = v_ref[j * blk:(j + 1) * blk, _head_cols(h)].T


def _sb_kernel(q_ref, k_ref, v_ref, g_ref, km_ref, vm_ref, tri_ref, o_ref, vt_ref, carry_ref,
               acc_ref, *, blk, heads):
    i = pl.program_id(2)
    hs = range(heads)

    @pl.when(i == 0)
    def _():
        _fill_vt(v_ref, vt_ref, blk=blk, heads=heads)

    def blocks(parts):
        units = [(part, h) for part in parts for h in hs]
        zs = [lax.dot_general(keys(h), q_ref[:, _head_cols(h)], _NT, preferred_element_type=F32)
              for (keys, _, _, _), h in units]
        log_betas, log_keeps, his, los = [], [], [], []
        for z, ((_, _, _, valid), _) in zip(zs, units):
            sp = jnp.log2(1.0 + jnp.exp2(-jnp.abs(z)))
            log_beta = jnp.minimum(z, 0.0) - sp
            log_keep = log_beta - z
            if valid is not None:
                log_keep = jnp.where(valid, log_keep, 0.0)
            hi = log_keep.astype(BF16)
            log_betas.append(log_beta)
            log_keeps.append(log_keep)
            his.append(hi)
            los.append((log_keep - hi.astype(F32)).astype(BF16))
        tails = [jnp.dot(tri, his[u], preferred_element_type=F32)
                 + jnp.dot(tri, los[u], preferred_element_type=F32)
                 for u, ((_, _, tri, _), _) in enumerate(units)]
        carry = {h: carry_ref[h] for h in hs}
        probs = []
        for u, ((_, _, _, valid), h) in enumerate(units):
            a = jnp.exp2(log_betas[u] + tails[u] + carry[h])
            if valid is not None:
                a = jnp.where(valid, a, 0.0)
            probs.append(a.astype(BF16))
            carry[h] = carry[h] + tails[u][0:1, :] + log_keeps[u][0:1, :]
        upd = [weighted_values(h, probs[u]) for u, ((_, weighted_values, _, _), h)
               in enumerate(units)]
        for u, (_, h) in enumerate(units):
            acc_ref[h] += upd[u]
        for h in hs:
            carry_ref[h] = carry[h]

    def real_part(j, valid):
        return (lambda h: k_ref[pl.ds(pl.multiple_of(j * blk, blk), blk), _head_cols(h)],
                lambda h, a: jnp.dot(vt_ref[h, j], a, preferred_element_type=F32),
                tri_ref[...], valid)

    def live():
        return jnp.max(carry_ref[...]) > SB_DEAD_LOG2

    carry_ref[...] = jnp.zeros_like(carry_ref)
    acc_ref[...] = jnp.zeros_like(acc_ref)
    r = lax.broadcasted_iota(jnp.int32, (blk, blk), 0)
    c = lax.broadcasted_iota(jnp.int32, (blk, blk), 1)
    @pl.when(i == 0)
    def _():
        blocks([real_part(0, r < c)])

    @pl.when(i > 0)
    def _():
        blocks([real_part(i, r < c), real_part(i - 1, None)])

    def older(state):
        jj, _ = state
        blocks([real_part(i - 1 - jj, None)])
        return jj + 1, jnp.logical_and(jj + 1 < i, live())

    lax.while_loop(lambda state: state[1], older,
                   (jnp.int32(1), jnp.logical_and(i > 1, live())))

    @pl.when(live())
    def _():
        blocks([(lambda h: km_ref[:, _head_cols(h)],
                 lambda h, a: lax.dot_general(vm_ref[:, _head_cols(h)], a, _TN,
                                              preferred_element_type=F32),
                 tri_ref[0:N_META, 0:N_META], None)])

    for h in hs:
        g = g_ref[:, _head_cols(h)].astype(F32)
        o_ref[:, _head_cols(h)] = (acc_ref[h].T * _silu(g)).astype(o_ref.dtype)


def _df_kernel(q_ref, k_ref, v_ref, g_ref, km_ref, vm_ref, lq1_ref, lk1_ref, lq2_ref, lk2_ref,
               sw_ref, o_ref, q2_ref, vt_ref, m_ref, acc_ref, *, blk, heads):
    i = pl.program_id(2)
    hs = range(heads)

    @pl.when(i == 0)
    def _():
        _fill_vt(v_ref, vt_ref, blk=blk, heads=heads)
        vt_ref[:, :, HEAD:, :] = jnp.ones_like(vt_ref[:, :, HEAD:, :])

    first = lax.broadcasted_iota(jnp.int32, (1, LANES), 1) < QK_DIM
    for h in hs:
        qf = q_ref[:, _head_cols(h)].astype(F32)
        q2_ref[h, 0:blk, :] = jnp.where(first, qf, 0.0).astype(BF16)
        q2_ref[h, blk:2 * blk, :] = jnp.where(first, 0.0, qf).astype(BF16)
    m_ref[...] = jnp.full_like(m_ref, -jnp.inf)
    acc_ref[...] = jnp.zeros_like(acc_ref)
    ones_meta = jnp.ones((N_META, ONES_ROWS), BF16)

    def block(j0, n_blocks, diag_row, with_meta):
        off = N_META if with_meta else 0
        size = off + n_blocks * blk
        rows = pl.ds(pl.multiple_of(j0 * blk, blk), n_blocks * blk)

        def keys(h):
            k = k_ref[rows, _head_cols(h)]
            return jnp.concatenate([km_ref[:, _head_cols(h)], k], axis=0) if with_meta else k

        if diag_row is not None:
            r = lax.broadcasted_iota(jnp.int32, (size, 2 * blk), 0)
            c = lax.broadcasted_iota(jnp.int32, (size, 2 * blk), 1)
            valid = r - (off + diag_row) <= jnp.where(c >= blk, c - blk, c)
        scores, alphas, probs = {}, {}, {}

        def score_matmul(h):
            scores[h] = lax.dot_general(keys(h), q2_ref[h], _NT, preferred_element_type=F32)

        def softmax(h):
            s = scores[h] if diag_row is None else jnp.where(valid, scores[h], -jnp.inf)
            m_new = jnp.maximum(m_ref[h], jnp.max(s, axis=0, keepdims=True))
            alphas[h] = jnp.exp2(m_ref[h] - m_new)
            m_ref[h] = m_new
            probs[h] = jnp.exp2(s - m_new).astype(BF16)

        def value_matmul(h):
            u = sum(jnp.dot(vt_ref[h, j0 + n], probs[h][off + n * blk:off + (n + 1) * blk],
                            preferred_element_type=F32) for n in range(n_blocks))
            if with_meta:
                vm1 = jnp.concatenate([vm_ref[:, _head_cols(h)], ones_meta], axis=1)
                u = u + lax.dot_general(vm1, probs[h][0:off], _TN, preferred_element_type=F32)
            acc_ref[h] = alphas[h] * acc_ref[h] + u

        for stage in (score_matmul, softmax, value_matmul):
            for h in hs:
                stage(h)

    @pl.loop(0, i // 2)
    def _(t):
        block(2 * t, 2, None, False)

    @pl.when(i % 2 == 1)
    def _():
        block(i - 1, 2, blk, True)

    @pl.when(i % 2 == 0)
    def _():
        block(i, 1, 0, True)

    lam = (jnp.exp(jnp.sum(lq1_ref[...] * lk1_ref[...], axis=1, keepdims=True))
           - jnp.exp(jnp.sum(lq2_ref[...] * lk2_ref[...], axis=1, keepdims=True))
           + LAMBDA_INIT)
    for h in hs:
        o = acc_ref[h, 0:HEAD] / acc_ref[h, HEAD:HEAD + 1]
        o = o[:, :blk] - lam * o[:, blk:]
        ms = jnp.mean(o * o, axis=0, keepdims=True)
        y = (o * lax.rsqrt(ms + SUBLN_EPS)).T * (sw_ref[...] * (1.0 - LAMBDA_INIT))
        g = g_ref[:, _head_cols(h)].astype(F32)
        o_ref[:, _head_cols(h)] = (y * _silu(g)).astype(o_ref.dtype)


def _attention(kernel_fn, q, k, v, g, meta_k, meta_v, extra, extra_specs, scratch, *, blk,
               heads, name):
    b, s, _ = q[0].shape
    groups = N_HEADS // heads
    width = heads * HEAD

    def col(seg):
        return lambda bi, h, i, seg=seg: (bi, i, seg * groups + h)

    def full(seg):
        return lambda bi, h, i, seg=seg: (bi, 0, seg * groups + h)

    def meta(seg):
        return lambda bi, h, i, seg=seg: (0, seg * groups + h)

    return pl.pallas_call(
        functools.partial(kernel_fn, blk=blk, heads=heads),
        grid=(b, groups, s // blk),
        in_specs=[
            pl.BlockSpec((None, blk, width), col(q[1])),
            pl.BlockSpec((None, s, width), full(k[1])),
            pl.BlockSpec((None, s, width), full(v[1])),
            pl.BlockSpec((None, blk, width), col(g[1])),
            pl.BlockSpec((N_META, width), meta(meta_k[1])),
            pl.BlockSpec((N_META, width), meta(meta_v[1])),
        ] + extra_specs,
        out_specs=pl.BlockSpec((None, blk, width), lambda bi, h, i: (bi, i, h)),
        out_shape=jax.ShapeDtypeStruct((b, s, SEG), BF16),
        scratch_shapes=scratch,
        compiler_params=pltpu.CompilerParams(
            dimension_semantics=("parallel", "parallel", "arbitrary"),
            vmem_limit_bytes=V7X_VMEM_LIMIT),
        name=name,
    )(q[0], k[0], v[0], g[0], meta_k[0], meta_v[0], *extra)


def _outproj_kernel(sb_ref, df_ref, wsb_ref, wdf_ref, x_ref, o_ref):
    y = (jnp.dot(sb_ref[...], wsb_ref[...], preferred_element_type=F32)
         + jnp.dot(df_ref[...], wdf_ref[...], preferred_element_type=F32))
    o_ref[...] = x_ref[...] + y


def _outproj(sb_mixed, df_mixed, w_out_bf16, x2d, *, tm):
    m, d = x2d.shape
    const = lambda i: (0, 0)
    return pl.pallas_call(
        _outproj_kernel,
        grid=(m // tm,),
        in_specs=[
            pl.BlockSpec((tm, SEG), lambda i: (i, 0)),
            pl.BlockSpec((tm, SEG), lambda i: (i, 0)),
            pl.BlockSpec((SEG, d), lambda i: (0, 0)),
            pl.BlockSpec((SEG, d), lambda i: (1, 0)),
            pl.BlockSpec((tm, d), lambda i: (i, 0)),
        ],
        out_specs=pl.BlockSpec((tm, d), lambda i: (i, 0)),
        out_shape=jax.ShapeDtypeStruct((m, d), x2d.dtype),
        compiler_params=pltpu.CompilerParams(
            dimension_semantics=("parallel",),
            vmem_limit_bytes=V7X_VMEM_LIMIT),
        name="outproj",
    )(sb_mixed, df_mixed, w_out_bf16, w_out_bf16, x2d)


def _rope_tables(t_all):
    pos = jnp.arange(t_all, dtype=F32)
    inv_freq = ROPE_THETA ** (-jnp.arange(0, 2 * ROT_HALF, 2, dtype=F32) / (2 * ROT_HALF))
    ang = pos[:, None] * inv_freq[None, :]
    cos, sin = jnp.cos(ang), jnp.sin(ang)
    lane = jnp.arange(LANES) % QK_DIM
    pair = lane % ROT_HALF
    cos_t = jnp.where(lane < 2 * ROT_HALF, cos[:, pair], 1.0)
    s1_t = jnp.where(lane < ROT_HALF, -sin[:, pair], 0.0)
    s2_t = jnp.where((lane >= ROT_HALF) & (lane < 2 * ROT_HALF), sin[:, pair], 0.0)
    return cos_t, s1_t, s2_t


def kernel(x, meta, norm_w, w_in, q_norm_w, k_norm_w, lambda_q1, lambda_k1, lambda_q2,
           lambda_k2, subln_w, w_out):
    b, s, d = x.shape
    assert norm_w.shape[0] == 1, "single-layer problem"
    blk = ATTN_BLOCK
    tm = min(INPROJ_ROWS, s)
    tm_out = min(OUTPROJ_ROWS, s)
    assert s % tm == 0 and s % (2 * blk) == 0 and s % tm_out == 0

    w_in_b = w_in[0].astype(BF16)
    w_out_b = w_out[0].astype(BF16)
    plain_w = jnp.ones((1, (N_SEG - 2) * SEG), F32).at[0, 0:SEG].set(SB_SCALE * LOG2_E)
    qk_w = jnp.stack([jnp.tile(q_norm_w[0] * (DF_SCALE * LOG2_E), SEG // QK_DIM),
                      jnp.tile(k_norm_w[0], SEG // QK_DIM)])
    cos_t, s1_t, s2_t = _rope_tables(N_META + s)

    x2d = x.reshape(b * s, d)
    qk, plain = _inproj(x2d, norm_w, w_in_b, plain_w, qk_w, cos_t[N_META:], s1_t[N_META:],
                        s2_t[N_META:], rows_per_seq=s, tm=tm)
    qk, plain = qk.reshape(b, s, -1), plain.reshape(b, s, -1)
    meta_qk, meta_plain = _inproj(meta, norm_w, w_in_b, plain_w, qk_w, cos_t[:N_META],
                                  s1_t[:N_META], s2_t[:N_META], rows_per_seq=N_META,
                                  tm=N_META)

    tri = (jnp.arange(blk)[None, :] > jnp.arange(blk)[:, None]).astype(BF16)
    const2 = lambda bi, h, i: (0, 0)
    sb_mixed = _attention(_sb_kernel, (plain, 0), (plain, 1), (plain, 2), (plain, 3),
                          (meta_plain, 1), (meta_plain, 2), [tri],
                          [pl.BlockSpec((blk, blk), const2)],
                          [pltpu.VMEM((SB_HEADS_PER_STEP, s // blk, HEAD, blk), BF16),
                           pltpu.VMEM((SB_HEADS_PER_STEP, 1, blk), F32),
                           pltpu.VMEM((SB_HEADS_PER_STEP, HEAD, blk), F32)],
                          blk=blk, heads=SB_HEADS_PER_STEP, name="sb_attn")
    vec = pl.BlockSpec((1, QK_DIM), const2)
    df_mixed = _attention(_df_kernel, (qk, 0), (qk, 1), (plain, 4), (plain, 5),
                          (meta_qk, 1), (meta_plain, 4),
                          [lambda_q1, lambda_k1, lambda_q2, lambda_k2, subln_w],
                          [vec, vec, vec, vec, pl.BlockSpec((1, HEAD), const2)],
                          [pltpu.VMEM((DF_HEADS_PER_STEP, 2 * blk, HEAD), BF16),
                           pltpu.VMEM((DF_HEADS_PER_STEP, s // blk, HEAD + ONES_ROWS, blk), BF16),
                           pltpu.VMEM((DF_HEADS_PER_STEP, 1, 2 * blk), F32),
                           pltpu.VMEM((DF_HEADS_PER_STEP, HEAD + ONES_ROWS, 2 * blk), F32)],
                          blk=blk, heads=DF_HEADS_PER_STEP, name="df_attn")
    out = _outproj(sb_mixed.reshape(b * s, SEG), df_mixed.reshape(b * s, SEG), w_out_b, x2d,
                   tm=tm_out)
    return out.reshape(b, s, d)
```

```python
import functools
import math

import jax
import jax.numpy as jnp
from jax import lax
from jax.experimental import pallas as pl
from jax.experimental.pallas import tpu as pltpu

N_META = 16
HEAD = 128
N_HEADS = 8
SEG = N_HEADS * HEAD
N_SEG = 8
SEG_DF_Q, SEG_DF_K = 4, 5
QK_DIM = 64
ROT_HALF = 8
ROPE_THETA = 500000.0
RMS_EPS = 1e-6
SUBLN_EPS = 1e-5
LAMBDA_INIT = 0.8 - 0.6 * math.exp(-0.3 * 0)
SB_SCALE = 1.0 / math.sqrt(HEAD)
SB_DEAD_LOG2 = -130.0
DF_SCALE = 1.0 / math.sqrt(QK_DIM)
LOG2_E = math.log2(math.e)
ATTN_BLOCK = 256
INPROJ_ROWS = 1024
OUTPROJ_ROWS = 512
INPROJ_CHUNK = 256
ONES_ROWS = 16
HEADS_PER_STEP = N_HEADS

LANES = 128
V7X_VMEM_LIMIT = 56 * 1024 * 1024

F32 = jnp.float32
BF16 = jnp.bfloat16

_NT = (((1,), (1,)), ((), ()))
_TN = (((0,), (0,)), ((), ()))


def _inproj_kernel(x_ref, nw_ref, w_ref, cw_ref, qkw_ref, cos_ref, s1_ref, s2_ref,
                   oqk_ref, opl_ref, u_ref, raw_ref):
    j = pl.program_id(1)

    @pl.when(j == 0)
    def _():
        x = x_ref[...]
        ms = jnp.mean(x * x, axis=-1, keepdims=True)
        u_ref[...] = (x * lax.rsqrt(ms + RMS_EPS) * nw_ref[...]).astype(BF16)

    def chunks():
        for c in range(w_ref.shape[1] // INPROJ_CHUNK):
            cols = slice(c * INPROJ_CHUNK, (c + 1) * INPROJ_CHUNK)
            yield cols, jnp.dot(u_ref[...], w_ref[:, cols], preferred_element_type=F32)

    def park(slot):
        for cols, acc in chunks():
            raw_ref[slot, :, cols] = acc

    def plain():
        for cols, acc in chunks():
            opl_ref[:, cols] = (acc * cw_ref[:, cols]).astype(opl_ref.dtype)

    def qk_epilogue(slot):
        first = lax.broadcasted_iota(jnp.int32, (1, LANES), 1) < QK_DIM
        cos, s1, s2 = cos_ref[...], s1_ref[...], s2_ref[...]
        for c in range(raw_ref.shape[2] // LANES):
            lanes = slice(c * LANES, (c + 1) * LANES)
            y = raw_ref[slot, :, lanes]
            y2 = y * y
            sa = jnp.sum(jnp.where(first, y2, 0.0), axis=1, keepdims=True)
            sb = jnp.sum(jnp.where(first, 0.0, y2), axis=1, keepdims=True)
            ms = jnp.where(first, sa, sb) * (1.0 / QK_DIM)
            yn = y * lax.rsqrt(ms + RMS_EPS) * qkw_ref[slot:slot + 1, lanes]
            out = (yn * cos + pltpu.roll(yn, LANES - ROT_HALF, 1) * s1
                   + pltpu.roll(yn, ROT_HALF, 1) * s2)
            oqk_ref[:, lanes] = out.astype(oqk_ref.dtype)

    @pl.when(j == 0)
    def _():
        park(0)

    @pl.when(j == 1)
    def _():
        park(1)
        qk_epilogue(0)

    @pl.when(j == 2)
    def _():
        plain()
        qk_epilogue(1)

    @pl.when(j > 2)
    def _():
        plain()


def _inproj_segment(j):
    return jnp.where(j == 0, SEG_DF_Q, jnp.where(j == 1, SEG_DF_K,
                                                 jnp.where(j < SEG_DF_Q + 2, j - 2, j)))


def _inproj(x2d, norm_w, w_bf16, plain_w, qk_w, cos_t, s1_t, s2_t, *, rows_per_seq, tm):
    m, d = x2d.shape
    tn = SEG
    n_pos_tiles = rows_per_seq // tm
    pos_map = lambda i, j: (i % n_pos_tiles, 0)
    plain_map = lambda i, j: (i, jnp.maximum(j - 2, 0))
    return pl.pallas_call(
        _inproj_kernel,
        grid=(m // tm, N_SEG),
        in_specs=[
            pl.BlockSpec((tm, d), lambda i, j: (i, 0)),
            pl.BlockSpec((1, d), lambda i, j: (0, 0)),
            pl.BlockSpec((d, tn), lambda i, j: (0, _inproj_segment(j))),
            pl.BlockSpec((1, tn), lambda i, j: (0, jnp.maximum(j - 2, 0))),
            pl.BlockSpec((2, tn), lambda i, j: (0, 0)),
            pl.BlockSpec((tm, LANES), pos_map),
            pl.BlockSpec((tm, LANES), pos_map),
            pl.BlockSpec((tm, LANES), pos_map),
        ],
        out_specs=[pl.BlockSpec((tm, tn), lambda i, j: (i, jnp.where(j < 2, 0, 1))),
                   pl.BlockSpec((tm, tn), plain_map)],
        out_shape=[jax.ShapeDtypeStruct((m, 2 * SEG), BF16),
                   jax.ShapeDtypeStruct((m, (N_SEG - 2) * SEG), BF16)],
        scratch_shapes=[pltpu.VMEM((tm, d), BF16), pltpu.VMEM((2, tm, tn), F32)],
        compiler_params=pltpu.CompilerParams(
            dimension_semantics=("parallel", "arbitrary"),
            vmem_limit_bytes=V7X_VMEM_LIMIT),
        name="inproj",
    )(x2d, norm_w, w_bf16, plain_w, qk_w, cos_t, s1_t, s2_t)


def _silu(g):
    return g * (1.0 / (1.0 + jnp.exp(-g)))


def _head_cols(h):
    return slice(h * HEAD, (h + 1) * HEAD)


def _fill_vt(v_ref, vt_ref, *, blk, heads):
    for h in range(heads):
        for j in range(v_ref.shape[0] // blk):
            vt_ref[h, j, 0:HEAD] = v_ref[j * blk:(j + 1) * blk, _head_cols(h)].T


def _sb_kernel(q_ref, k_ref, v_ref, g_ref, km_ref, vm_ref, tri_ref, o_ref, vt_ref, carry_ref,
               acc_ref, *, blk, heads):
    i = pl.program_id(2)
    hs = range(heads)

    @pl.when(i == 0)
    def _():
        _fill_vt(v_ref, vt_ref, blk=blk, heads=heads)

    def blocks(parts):
        units = [(part, h) for part in parts for h in hs]
        zs = [lax.dot_general(keys(h), q_ref[:, _head_cols(h)], _NT, preferred_element_type=F32)
              for (keys, _, _, _), h in units]
        log_betas, log_keeps, his, los = [], [], [], []
        for z, ((_, _, _, valid), _) in zip(zs, units):
            sp = jnp.log2(1.0 + jnp.exp2(-jnp.abs(z)))
            log_beta = jnp.minimum(z, 0.0) - sp
            log_keep = log_beta - z
            if valid is not None:
                log_keep = jnp.where(valid, log_keep, 0.0)
            hi = log_keep.astype(BF16)
            log_betas.append(log_beta)
            log_keeps.append(log_keep)
            his.append(hi)
            los.append((log_keep - hi.astype(F32)).astype(BF16))
        tails = [jnp.dot(tri, his[u], preferred_element_type=F32)
                 + jnp.dot(tri, los[u], preferred_element_type=F32)
                 for u, ((_, _, tri, _), _) in enumerate(units)]
        carry = {h: carry_ref[h] for h in hs}
        probs = []
        for u, ((_, _, _, valid), h) in enumerate(units):
            a = jnp.exp2(log_betas[u] + tails[u] + carry[h])
            if valid is not None:
                a = jnp.where(valid, a, 0.0)
            probs.append(a.astype(BF16))
            carry[h] = carry[h] + tails[u][0:1, :] + log_keeps[u][0:1, :]
        upd = [weighted_values(h, probs[u]) for u, ((_, weighted_values, _, _), h)
               in enumerate(units)]
        for u, (_, h) in enumerate(units):
            acc_ref[h] += upd[u]
        for h in hs:
            carry_ref[h] = carry[h]

    def real_part(j, valid):
        return (lambda h: k_ref[pl.ds(pl.multiple_of(j * blk, blk), blk), _head_cols(h)],
                lambda h, a: jnp.dot(vt_ref[h, j], a, preferred_element_type=F32),
                tri_ref[...], valid)

    def live():
        return jnp.max(carry_ref[...]) > SB_DEAD_LOG2

    carry_ref[...] = jnp.zeros_like(carry_ref)
    acc_ref[...] = jnp.zeros_like(acc_ref)
    r = lax.broadcasted_iota(jnp.int32, (blk, blk), 0)
    c = lax.broadcasted_iota(jnp.int32, (blk, blk), 1)
    @pl.when(i == 0)
    def _():
        blocks([real_part(0, r < c)])

    @pl.when(i > 0)
    def _():
        blocks([real_part(i, r < c), real_part(i - 1, None)])

    def older(state):
        jj, _ = state
        blocks([real_part(i - 1 - jj, None)])
        return jj + 1, jnp.logical_and(jj + 1 < i, live())

    lax.while_loop(lambda state: state[1], older,
                   (jnp.int32(1), jnp.logical_and(i > 1, live())))

    @pl.when(live())
    def _():
        blocks([(lambda h: km_ref[:, _head_cols(h)],
                 lambda h, a: lax.dot_general(vm_ref[:, _head_cols(h)], a, _TN,
                                              preferred_element_type=F32),
                 tri_ref[0:N_META, 0:N_META], None)])

    for h in hs:
        g = g_ref[:, _head_cols(h)].astype(F32)
        o_ref[:, _head_cols(h)] = (acc_ref[h].T * _silu(g)).astype(o_ref.dtype)


def _df_kernel(q_ref, k_ref, v_ref, g_ref, km_ref, vm_ref, lq1_ref, lk1_ref, lq2_ref, lk2_ref,
               sw_ref, o_ref, q2_ref, vt_ref, m_ref, acc_ref, *, blk, heads):
    i = pl.program_id(2)
    hs = range(heads)

    @pl.when(i == 0)
    def _():
        _fill_vt(v_ref, vt_ref, blk=blk, heads=heads)
        vt_ref[:, :, HEAD:, :] = jnp.ones_like(vt_ref[:, :, HEAD:, :])

    first = lax.broadcasted_iota(jnp.int32, (1, LANES), 1) < QK_DIM
    for h in hs:
        qf = q_ref[:, _head_cols(h)].astype(F32)
        q2_ref[h, 0:blk, :] = jnp.where(first, qf, 0.0).astype(BF16)
        q2_ref[h, blk:2 * blk, :] = jnp.where(first, 0.0, qf).astype(BF16)
    m_ref[...] = jnp.full_like(m_ref, -jnp.inf)
    acc_ref[...] = jnp.zeros_like(acc_ref)
    ones_meta = jnp.ones((N_META, ONES_ROWS), BF16)

    def block(j, own):
        off = N_META if own else 0
        rows = pl.ds(pl.multiple_of(j * blk, blk), blk)

        def keys(h):
            k = k_ref[rows, _head_cols(h)]
            return jnp.concatenate([km_ref[:, _head_cols(h)], k], axis=0) if own else k

        if own:
            r = lax.broadcasted_iota(jnp.int32, (off + blk, 2 * blk), 0)
            c = lax.broadcasted_iota(jnp.int32, (off + blk, 2 * blk), 1)
            valid = r - off <= jnp.where(c >= blk, c - blk, c)
        scores, alphas, probs = {}, {}, {}

        def score_matmul(h):
            scores[h] = lax.dot_general(keys(h), q2_ref[h], _NT, preferred_element_type=F32)

        def softmax(h):
            s = jnp.where(valid, scores[h], -jnp.inf) if own else scores[h]
            m_new = jnp.maximum(m_ref[h], jnp.max(s, axis=0, keepdims=True))
            alphas[h] = jnp.exp2(m_ref[h] - m_new)
            m_ref[h] = m_new
            probs[h] = jnp.exp2(s - m_new).astype(BF16)

        def value_matmul(h):
            u = jnp.dot(vt_ref[h, j], probs[h][off:off + blk], preferred_element_type=F32)
            if own:
                vm1 = jnp.concatenate([vm_ref[:, _head_cols(h)], ones_meta], axis=1)
                u = u + lax.dot_general(vm1, probs[h][0:off], _TN, preferred_element_type=F32)
            acc_ref[h] = alphas[h] * acc_ref[h] + u

        for stage in (score_matmul, softmax, value_matmul):
            for h in hs:
                stage(h)

    @pl.loop(0, i)
    def _(t):
        block(t, False)

    block(i, True)

    lam = (jnp.exp(jnp.sum(lq1_ref[...] * lk1_ref[...], axis=1, keepdims=True))
           - jnp.exp(jnp.sum(lq2_ref[...] * lk2_ref[...], axis=1, keepdims=True))
           + LAMBDA_INIT)
    for h in hs:
        o = acc_ref[h, 0:HEAD] * (1.0 / acc_ref[h, HEAD:HEAD + 1])
        o = o[:, :blk] - lam * o[:, blk:]
        ms = jnp.mean(o * o, axis=0, keepdims=True)
        y = (o * lax.rsqrt(ms + SUBLN_EPS)).T * (sw_ref[...] * (1.0 - LAMBDA_INIT))
        g = g_ref[:, _head_cols(h)].astype(F32)
        o_ref[:, _head_cols(h)] = (y * _silu(g)).astype(o_ref.dtype)


def _attention(kernel_fn, q, k, v, g, meta_k, meta_v, extra, extra_specs, scratch, *, blk,
               heads, name):
    b, s, _ = q[0].shape
    groups = N_HEADS // heads
    width = heads * HEAD

    def col(seg):
        return lambda bi, h, i, seg=seg: (bi, i, seg * groups + h)

    def full(seg):
        return lambda bi, h, i, seg=seg: (bi, 0, seg * groups + h)

    def meta(seg):
        return lambda bi, h, i, seg=seg: (0, seg * groups + h)

    return pl.pallas_call(
        functools.partial(kernel_fn, blk=blk, heads=heads),
        grid=(b, groups, s // blk),
        in_specs=[
            pl.BlockSpec((None, blk, width), col(q[1])),
            pl.BlockSpec((None, s, width), full(k[1])),
            pl.BlockSpec((None, s, width), full(v[1])),
            pl.BlockSpec((None, blk, width), col(g[1])),
            pl.BlockSpec((N_META, width), meta(meta_k[1])),
            pl.BlockSpec((N_META, width), meta(meta_v[1])),
        ] + extra_specs,
        out_specs=pl.BlockSpec((None, blk, width), lambda bi, h, i: (bi, i, h)),
        out_shape=jax.ShapeDtypeStruct((b, s, SEG), BF16),
        scratch_shapes=scratch,
        compiler_params=pltpu.CompilerParams(
            dimension_semantics=("parallel", "parallel", "arbitrary"),
            vmem_limit_bytes=V7X_VMEM_LIMIT),
        name=name,
    )(q[0], k[0], v[0], g[0], meta_k[0], meta_v[0], *extra)


def _outproj_kernel(sb_ref, df_ref, wsb_ref, wdf_ref, x_ref, o_ref):
    y = (jnp.dot(sb_ref[...], wsb_ref[...], preferred_element_type=F32)
         + jnp.dot(df_ref[...], wdf_ref[...], preferred_element_type=F32))
    o_ref[...] = x_ref[...] + y


def _outproj(sb_mixed, df_mixed, w_out_bf16, x2d, *, tm):
    m, d = x2d.shape
    const = lambda i: (0, 0)
    return pl.pallas_call(
        _outproj_kernel,
        grid=(m // tm,),
        in_specs=[
            pl.BlockSpec((tm, SEG), lambda i: (i, 0)),
            pl.BlockSpec((tm, SEG), lambda i: (i, 0)),
            pl.BlockSpec((SEG, d), lambda i: (0, 0)),
            pl.BlockSpec((SEG, d), lambda i: (1, 0)),
            pl.BlockSpec((tm, d), lambda i: (i, 0)),
        ],
        out_specs=pl.BlockSpec((tm, d), lambda i: (i, 0)),
        out_shape=jax.ShapeDtypeStruct((m, d), x2d.dtype),
        compiler_params=pltpu.CompilerParams(
            dimension_semantics=("parallel",),
            vmem_limit_bytes=V7X_VMEM_LIMIT),
        name="outproj",
    )(sb_mixed, df_mixed, w_out_bf16, w_out_bf16, x2d)


def _rope_tables(t_all):
    pos = jnp.arange(t_all, dtype=F32)
    inv_freq = ROPE_THETA ** (-jnp.arange(0, 2 * ROT_HALF, 2, dtype=F32) / (2 * ROT_HALF))
    ang = pos[:, None] * inv_freq[None, :]
    cos, sin = jnp.cos(ang), jnp.sin(ang)
    lane = jnp.arange(LANES) % QK_DIM
    pair = lane % ROT_HALF
    cos_t = jnp.where(lane < 2 * ROT_HALF, cos[:, pair], 1.0)
    s1_t = jnp.where(lane < ROT_HALF, -sin[:, pair], 0.0)
    s2_t = jnp.where((lane >= ROT_HALF) & (lane < 2 * ROT_HALF), sin[:, pair], 0.0)
    return cos_t, s1_t, s2_t


def kernel(x, meta, norm_w, w_in, q_norm_w, k_norm_w, lambda_q1, lambda_k1, lambda_q2,
           lambda_k2, subln_w, w_out):
    b, s, d = x.shape
    assert norm_w.shape[0] == 1, "single-layer problem"
    blk = ATTN_BLOCK
    tm = min(INPROJ_ROWS, s)
    tm_out = min(OUTPROJ_ROWS, s)
    assert s % tm == 0 and s % blk == 0 and s % tm_out == 0

    w_in_b = w_in[0].astype(BF16)
    w_out_b = w_out[0].astype(BF16)
    plain_w = jnp.ones((1, (N_SEG - 2) * SEG), F32).at[0, 0:SEG].set(SB_SCALE * LOG2_E)
    qk_w = jnp.stack([jnp.tile(q_norm_w[0] * (DF_SCALE * LOG2_E), SEG // QK_DIM),
                      jnp.tile(k_norm_w[0], SEG // QK_DIM)])
    cos_t, s1_t, s2_t = _rope_tables(N_META + s)

    x2d = x.reshape(b * s, d)
    qk, plain = _inproj(x2d, norm_w, w_in_b, plain_w, qk_w, cos_t[N_META:], s1_t[N_META:],
                        s2_t[N_META:], rows_per_seq=s, tm=tm)
    qk, plain = qk.reshape(b, s, -1), plain.reshape(b, s, -1)
    meta_qk, meta_plain = _inproj(meta, norm_w, w_in_b, plain_w, qk_w, cos_t[:N_META],
                                  s1_t[:N_META], s2_t[:N_META], rows_per_seq=N_META,
                                  tm=N_META)

    tri = (jnp.arange(blk)[None, :] > jnp.arange(blk)[:, None]).astype(BF16)
    const2 = lambda bi, h, i: (0, 0)
    sb_mixed = _attention(_sb_kernel, (plain, 0), (plain, 1), (plain, 2), (plain, 3),
                          (meta_plain, 1), (meta_plain, 2), [tri],
                          [pl.BlockSpec((blk, blk), const2)],
                          [pltpu.VMEM((HEADS_PER_STEP, s // blk, HEAD, blk), BF16),
                           pltpu.VMEM((HEADS_PER_STEP, 1, blk), F32),
                           pltpu.VMEM((HEADS_PER_STEP, HEAD, blk), F32)],
                          blk=blk, heads=HEADS_PER_STEP, name="sb_attn")
    vec = pl.BlockSpec((1, QK_DIM), const2)
    df_mixed = _attention(_df_kernel, (qk, 0), (qk, 1), (plain, 4), (plain, 5),
                          (meta_qk, 1), (meta_plain, 4),
                          [lambda_q1, lambda_k1, lambda_q2, lambda_k2, subln_w],
                          [vec, vec, vec, vec, pl.BlockSpec((1, HEAD), const2)],
                          [pltpu.VMEM((HEADS_PER_STEP, 2 * blk, HEAD), BF16),
                           pltpu.VMEM((HEADS_PER_STEP, s // blk, HEAD + ONES_ROWS, blk), BF16),
                           pltpu.VMEM((HEADS_PER_STEP, 1, 2 * blk), F32),
                           pltpu.VMEM((HEADS_PER_STEP, HEAD + ONES_ROWS, 2 * blk), F32)],
                          blk=blk, heads=HEADS_PER_STEP, name="df_attn")
    out = _outproj(sb_mixed.reshape(b * s, SEG), df_mixed.reshape(b * s, SEG), w_out_b, x2d,
                   tm=tm_out)
    return out.reshape(b, s, d)
```

```python
import functools
import math

import jax
import jax.numpy as jnp
from jax import lax
from jax.experimental import pallas as pl
from jax.experimental.pallas import tpu as pltpu

N_META = 16
HEAD = 128
N_HEADS = 8
SEG = N_HEADS * HEAD
N_SEG = 8
SEG_DF_Q, SEG_DF_K = 4, 5
QK_DIM = 64
ROT_HALF = 8
ROPE_THETA = 500000.0
RMS_EPS = 1e-6
SUBLN_EPS = 1e-5
LAMBDA_INIT = 0.8 - 0.6 * math.exp(-0.3 * 0)
SB_SCALE = 1.0 / math.sqrt(HEAD)
SB_DEAD_LOG2 = -130.0
DF_SCALE = 1.0 / math.sqrt(QK_DIM)
LOG2_E = math.log2(math.e)
ATTN_BLOCK = 256
INPROJ_ROWS = 1024
OUTPROJ_ROWS = 512
INPROJ_CHUNK = 256
ONES_ROWS = 16
HEADS_PER_STEP = N_HEADS
DF_SHIFT_MARGIN = 1.02
DF_FIXED_SHIFT_RANGE = 120.0

LANES = 128
V7X_VMEM_LIMIT = 56 * 1024 * 1024

F32 = jnp.float32
BF16 = jnp.bfloat16

_NT = (((1,), (1,)), ((), ()))
_TN = (((0,), (0,)), ((), ()))


def _inproj_kernel(x_ref, nw_ref, w_ref, cw_ref, qkw_ref, cos_ref, s1_ref, s2_ref,
                   oqk_ref, opl_ref, u_ref, raw_ref):
    j = pl.program_id(1)

    @pl.when(j == 0)
    def _():
        x = x_ref[...]
        ms = jnp.mean(x * x, axis=-1, keepdims=True)
        u_ref[...] = (x * lax.rsqrt(ms + RMS_EPS) * nw_ref[...]).astype(BF16)

    def chunks():
        for c in range(w_ref.shape[1] // INPROJ_CHUNK):
            cols = slice(c * INPROJ_CHUNK, (c + 1) * INPROJ_CHUNK)
            yield cols, jnp.dot(u_ref[...], w_ref[:, cols], preferred_element_type=F32)

    def park(slot):
        for cols, acc in chunks():
            raw_ref[slot, :, cols] = acc

    def plain():
        for cols, acc in chunks():
            opl_ref[:, cols] = (acc * cw_ref[:, cols]).astype(opl_ref.dtype)

    def qk_epilogue(slot):
        first = lax.broadcasted_iota(jnp.int32, (1, LANES), 1) < QK_DIM
        cos, s1, s2 = cos_ref[...], s1_ref[...], s2_ref[...]
        for c in range(raw_ref.shape[2] // LANES):
            lanes = slice(c * LANES, (c + 1) * LANES)
            y = raw_ref[slot, :, lanes]
            y2 = y * y
            sa = jnp.sum(jnp.where(first, y2, 0.0), axis=1, keepdims=True)
            sb = jnp.sum(jnp.where(first, 0.0, y2), axis=1, keepdims=True)
            ms = jnp.where(first, sa, sb) * (1.0 / QK_DIM)
            yn = y * lax.rsqrt(ms + RMS_EPS) * qkw_ref[slot:slot + 1, lanes]
            out = (yn * cos + pltpu.roll(yn, LANES - ROT_HALF, 1) * s1
                   + pltpu.roll(yn, ROT_HALF, 1) * s2)
            oqk_ref[:, lanes] = out.astype(oqk_ref.dtype)

    @pl.when(j == 0)
    def _():
        park(0)

    @pl.when(j == 1)
    def _():
        park(1)
        qk_epilogue(0)

    @pl.when(j == 2)
    def _():
        plain()
        qk_epilogue(1)

    @pl.when(j > 2)
    def _():
        plain()


def _inproj_segment(j):
    return jnp.where(j == 0, SEG_DF_Q, jnp.where(j == 1, SEG_DF_K,
                                                 jnp.where(j < SEG_DF_Q + 2, j - 2, j)))


def _inproj(x2d, norm_w, w_bf16, plain_w, qk_w, cos_t, s1_t, s2_t, *, rows_per_seq, tm):
    m, d = x2d.shape
    tn = SEG
    n_pos_tiles = rows_per_seq // tm
    pos_map = lambda i, j: (i % n_pos_tiles, 0)
    plain_map = lambda i, j: (i, jnp.maximum(j - 2, 0))
    return pl.pallas_call(
        _inproj_kernel,
        grid=(m // tm, N_SEG),
        in_specs=[
            pl.BlockSpec((tm, d), lambda i, j: (i, 0)),
            pl.BlockSpec((1, d), lambda i, j: (0, 0)),
            pl.BlockSpec((d, tn), lambda i, j: (0, _inproj_segment(j))),
            pl.BlockSpec((1, tn), lambda i, j: (0, jnp.maximum(j - 2, 0))),
            pl.BlockSpec((2, tn), lambda i, j: (0, 0)),
            pl.BlockSpec((tm, LANES), pos_map),
            pl.BlockSpec((tm, LANES), pos_map),
            pl.BlockSpec((tm, LANES), pos_map),
        ],
        out_specs=[pl.BlockSpec((tm, tn), lambda i, j: (i, jnp.where(j < 2, 0, 1))),
                   pl.BlockSpec((tm, tn), plain_map)],
        out_shape=[jax.ShapeDtypeStruct((m, 2 * SEG), BF16),
                   jax.ShapeDtypeStruct((m, (N_SEG - 2) * SEG), BF16)],
        scratch_shapes=[pltpu.VMEM((tm, d), BF16), pltpu.VMEM((2, tm, tn), F32)],
        compiler_params=pltpu.CompilerParams(
            dimension_semantics=("parallel", "arbitrary"),
            vmem_limit_bytes=V7X_VMEM_LIMIT),
        name="inproj",
    )(x2d, norm_w, w_bf16, plain_w, qk_w, cos_t, s1_t, s2_t)


def _silu(g):
    return g * (1.0 / (1.0 + jnp.exp(-g)))


def _head_cols(h):
    return slice(h * HEAD, (h + 1) * HEAD)


def _fill_vt(v_ref, vt_ref, *, blk, heads):
    for h in range(heads):
        for j in range(v_ref.shape[0] // blk):
            vt_ref[h, j, 0:HEAD] = v_ref[j * blk:(j + 1) * blk, _head_cols(h)].T


def _sb_kernel(q_ref, k_ref, v_ref, g_ref, km_ref, vm_ref, tri_ref, o_ref, vt_ref, carry_ref,
               acc_ref, *, blk, heads):
    i = pl.program_id(2)
    hs = range(heads)

    @pl.when(i == 0)
    def _():
        _fill_vt(v_ref, vt_ref, blk=blk, heads=heads)

    def blocks(parts):
        units = [(part, h) for part in parts for h in hs]
        zs = [lax.dot_general(keys(h), q_ref[:, _head_cols(h)], _NT, preferred_element_type=F32)
              for (keys, _, _, _), h in units]
        log_betas, log_keeps, his, los = [], [], [], []
        for z, ((_, _, _, valid), _) in zip(zs, units):
            sp = jnp.log2(1.0 + jnp.exp2(-jnp.abs(z)))
            log_beta = jnp.minimum(z, 0.0) - sp
            log_keep = log_beta - z
            if valid is not None:
                log_keep = jnp.where(valid, log_keep, 0.0)
            hi = log_keep.astype(BF16)
            log_betas.append(log_beta)
            log_keeps.append(log_keep)
            his.append(hi)
            los.append((log_keep - hi.astype(F32)).astype(BF16))
        tails = [jnp.dot(tri, his[u], preferred_element_type=F32)
                 + jnp.dot(tri, los[u], preferred_element_type=F32)
                 for u, ((_, _, tri, _), _) in enumerate(units)]
        carry = {h: carry_ref[h] for h in hs}
        probs = []
        for u, ((_, _, _, valid), h) in enumerate(units):
            a = jnp.exp2(log_betas[u] + tails[u] + carry[h])
            if valid is not None:
                a = jnp.where(valid, a, 0.0)
            probs.append(a.astype(BF16))
            carry[h] = carry[h] + tails[u][0:1, :] + log_keeps[u][0:1, :]
        upd = [weighted_values(h, probs[u]) for u, ((_, weighted_values, _, _), h)
               in enumerate(units)]
        for u, (_, h) in enumerate(units):
            acc_ref[h] += upd[u]
        for h in hs:
            carry_ref[h] = carry[h]

    def real_part(j, valid):
        return (lambda h: k_ref[pl.ds(pl.multiple_of(j * blk, blk), blk), _head_cols(h)],
                lambda h, a: jnp.dot(vt_ref[h, j], a, preferred_element_type=F32),
                tri_ref[...], valid)

    def live():
        return jnp.max(carry_ref[...]) > SB_DEAD_LOG2

    carry_ref[...] = jnp.zeros_like(carry_ref)
    acc_ref[...] = jnp.zeros_like(acc_ref)
    r = lax.broadcasted_iota(jnp.int32, (blk, blk), 0)
    c = lax.broadcasted_iota(jnp.int32, (blk, blk), 1)
    @pl.when(i == 0)
    def _():
        blocks([real_part(0, r < c)])

    @pl.when(i > 0)
    def _():
        blocks([real_part(i, r < c), real_part(i - 1, None)])

    def older(state):
        jj, _ = state
        blocks([real_part(i - 1 - jj, None)])
        return jj + 1, jnp.logical_and(jj + 1 < i, live())

    lax.while_loop(lambda state: state[1], older,
                   (jnp.int32(1), jnp.logical_and(i > 1, live())))

    @pl.when(live())
    def _():
        blocks([(lambda h: km_ref[:, _head_cols(h)],
                 lambda h, a: lax.dot_general(vm_ref[:, _head_cols(h)], a, _TN,
                                              preferred_element_type=F32),
                 tri_ref[0:N_META, 0:N_META], None)])

    for h in hs:
        g = g_ref[:, _head_cols(h)].astype(F32)
        o_ref[:, _head_cols(h)] = (acc_ref[h].T * _silu(g)).astype(o_ref.dtype)


def _df_kernel(q_ref, k_ref, v_ref, g_ref, km_ref, vm_ref, lq1_ref, lk1_ref, lq2_ref, lk2_ref,
               sw_ref, o_ref, q2_ref, vt_ref, kmax_ref, shift_ref, m_ref, acc_ref, *, blk, heads):
    i = pl.program_id(2)
    hs = range(heads)
    first = lax.broadcasted_iota(jnp.int32, (1, LANES), 1) < QK_DIM

    @pl.when(i == 0)
    def _():
        _fill_vt(v_ref, vt_ref, blk=blk, heads=heads)
        vt_ref[:, :, HEAD:, :] = jnp.ones_like(vt_ref[:, :, HEAD:, :])
        same = (lax.broadcasted_iota(jnp.int32, (LANES, LANES), 0) < QK_DIM) == first
        group_sum = jnp.where(same, 1.0, 0.0).astype(BF16)
        for h in hs:
            best = jnp.zeros((1, LANES), F32)
            for ref in (k_ref, km_ref):
                kf = ref[:, _head_cols(h)].astype(F32)
                norms = jnp.dot((kf * kf).astype(BF16), group_sum, preferred_element_type=F32)
                best = jnp.maximum(best, jnp.max(norms, axis=0, keepdims=True))
            kmax_ref[h] = best

    col = lax.broadcasted_iota(jnp.int32, (1, 2 * blk), 1)
    ones_rows = jnp.ones((8, LANES), BF16)
    for h in hs:
        qf = q_ref[:, _head_cols(h)].astype(F32)
        q2_ref[h, 0:blk, :] = jnp.where(first, qf, 0.0).astype(BF16)
        q2_ref[h, blk:2 * blk, :] = jnp.where(first, 0.0, qf).astype(BF16)
        q2f = q2_ref[h].astype(F32)
        qn2 = lax.dot_general(ones_rows, (q2f * q2f).astype(BF16), _NT,
                              preferred_element_type=F32)[0:1, :]
        kn2 = jnp.where(col < blk, kmax_ref[h, :, 0:1], kmax_ref[h, :, QK_DIM:QK_DIM + 1])
        shift_ref[h] = jnp.sqrt(qn2 * kn2) * DF_SHIFT_MARGIN
    acc_ref[...] = jnp.zeros_like(acc_ref)
    ones_meta = jnp.ones((N_META, ONES_ROWS), BF16)

    def block(j, own, fixed_shift):
        off = N_META if own else 0
        rows = pl.ds(pl.multiple_of(j * blk, blk), blk)

        def keys(h):
            k = k_ref[rows, _head_cols(h)]
            return jnp.concatenate([km_ref[:, _head_cols(h)], k], axis=0) if own else k

        if own:
            r = lax.broadcasted_iota(jnp.int32, (off + blk, 2 * blk), 0)
            c = lax.broadcasted_iota(jnp.int32, (off + blk, 2 * blk), 1)
            valid = r - off <= jnp.where(c >= blk, c - blk, c)
        scores, alphas, probs = {}, {}, {}

        def score_matmul(h):
            scores[h] = lax.dot_general(keys(h), q2_ref[h], _NT, preferred_element_type=F32)

        def softmax(h):
            s = jnp.where(valid, scores[h], -jnp.inf) if own else scores[h]
            if fixed_shift:
                probs[h] = jnp.exp2(s - shift_ref[h]).astype(BF16)
                return
            m_new = jnp.maximum(m_ref[h], jnp.max(s, axis=0, keepdims=True))
            alphas[h] = jnp.exp2(m_ref[h] - m_new)
            m_ref[h] = m_new
            probs[h] = jnp.exp2(s - m_new).astype(BF16)

        def value_matmul(h):
            u = jnp.dot(vt_ref[h, j], probs[h][off:off + blk], preferred_element_type=F32)
            if own:
                vm1 = jnp.concatenate([vm_ref[:, _head_cols(h)], ones_meta], axis=1)
                u = u + lax.dot_general(vm1, probs[h][0:off], _TN, preferred_element_type=F32)
            acc_ref[h] = acc_ref[h] + u if fixed_shift else alphas[h] * acc_ref[h] + u

        for stage in (score_matmul, softmax, value_matmul):
            for h in hs:
                stage(h)

    def all_blocks(fixed_shift):
        @pl.loop(0, i)
        def _(t):
            block(t, False, fixed_shift)

        block(i, True, fixed_shift)

    safe = 2.0 * jnp.max(shift_ref[...]) < DF_FIXED_SHIFT_RANGE

    @pl.when(safe)
    def _():
        all_blocks(True)

    @pl.when(jnp.logical_not(safe))
    def _():
        m_ref[...] = jnp.full_like(m_ref, -jnp.inf)
        all_blocks(False)

    lam = (jnp.exp(jnp.sum(lq1_ref[...] * lk1_ref[...], axis=1, keepdims=True))
           - jnp.exp(jnp.sum(lq2_ref[...] * lk2_ref[...], axis=1, keepdims=True))
           + LAMBDA_INIT)
    for h in hs:
        o = acc_ref[h, 0:HEAD] * (1.0 / acc_ref[h, HEAD:HEAD + 1])
        o = o[:, :blk] - lam * o[:, blk:]
        ms = jnp.mean(o * o, axis=0, keepdims=True)
        y = (o * lax.rsqrt(ms + SUBLN_EPS)).T * (sw_ref[...] * (1.0 - LAMBDA_INIT))
        g = g_ref[:, _head_cols(h)].astype(F32)
        o_ref[:, _head_cols(h)] = (y * _silu(g)).astype(o_ref.dtype)


def _attention(kernel_fn, q, k, v, g, meta_k, meta_v, extra, extra_specs, scratch, *, blk,
               heads, name):
    b, s, _ = q[0].shape
    groups = N_HEADS // heads
    width = heads * HEAD

    def col(seg):
        return lambda bi, h, i, seg=seg: (bi, i, seg * groups + h)

    def full(seg):
        return lambda bi, h, i, seg=seg: (bi, 0, seg * groups + h)

    def meta(seg):
        return lambda bi, h, i, seg=seg: (0, seg * groups + h)

    return pl.pallas_call(
        functools.partial(kernel_fn, blk=blk, heads=heads),
        grid=(b, groups, s // blk),
        in_specs=[
            pl.BlockSpec((None, blk, width), col(q[1])),
            pl.BlockSpec((None, s, width), full(k[1])),
            pl.BlockSpec((None, s, width), full(v[1])),
            pl.BlockSpec((None, blk, width), col(g[1])),
            pl.BlockSpec((N_META, width), meta(meta_k[1])),
            pl.BlockSpec((N_META, width), meta(meta_v[1])),
        ] + extra_specs,
        out_specs=pl.BlockSpec((None, blk, width), lambda bi, h, i: (bi, i, h)),
        out_shape=jax.ShapeDtypeStruct((b, s, SEG), BF16),
        scratch_shapes=scratch,
        compiler_params=pltpu.CompilerParams(
            dimension_semantics=("parallel", "parallel", "arbitrary"),
            vmem_limit_bytes=V7X_VMEM_LIMIT),
        name=name,
    )(q[0], k[0], v[0], g[0], meta_k[0], meta_v[0], *extra)


def _outproj_kernel(sb_ref, df_ref, wsb_ref, wdf_ref, x_ref, o_ref):
    y = (jnp.dot(sb_ref[...], wsb_ref[...], preferred_element_type=F32)
         + jnp.dot(df_ref[...], wdf_ref[...], preferred_element_type=F32))
    o_ref[...] = x_ref[...] + y


def _outproj(sb_mixed, df_mixed, w_out_bf16, x2d, *, tm):
    m, d = x2d.shape
    const = lambda i: (0, 0)
    return pl.pallas_call(
        _outproj_kernel,
        grid=(m // tm,),
        in_specs=[
            pl.BlockSpec((tm, SEG), lambda i: (i, 0)),
            pl.BlockSpec((tm, SEG), lambda i: (i, 0)),
            pl.BlockSpec((SEG, d), lambda i: (0, 0)),
            pl.BlockSpec((SEG, d), lambda i: (1, 0)),
            pl.BlockSpec((tm, d), lambda i: (i, 0)),
        ],
        out_specs=pl.BlockSpec((tm, d), lambda i: (i, 0)),
        out_shape=jax.ShapeDtypeStruct((m, d), x2d.dtype),
        compiler_params=pltpu.CompilerParams(
            dimension_semantics=("parallel",),
            vmem_limit_bytes=V7X_VMEM_LIMIT),
        name="outproj",
    )(sb_mixed, df_mixed, w_out_bf16, w_out_bf16, x2d)


def _rope_tables(t_all):
    pos = jnp.arange(t_all, dtype=F32)
    inv_freq = ROPE_THETA ** (-jnp.arange(0, 2 * ROT_HALF, 2, dtype=F32) / (2 * ROT_HALF))
    ang = pos[:, None] * inv_freq[None, :]
    cos, sin = jnp.cos(ang), jnp.sin(ang)
    lane = jnp.arange(LANES) % QK_DIM
    pair = lane % ROT_HALF
    cos_t = jnp.where(lane < 2 * ROT_HALF, cos[:, pair], 1.0)
    s1_t = jnp.where(lane < ROT_HALF, -sin[:, pair], 0.0)
    s2_t = jnp.where((lane >= ROT_HALF) & (lane < 2 * ROT_HALF), sin[:, pair], 0.0)
    return cos_t, s1_t, s2_t


def kernel(x, meta, norm_w, w_in, q_norm_w, k_norm_w, lambda_q1, lambda_k1, lambda_q2,
           lambda_k2, subln_w, w_out):
    b, s, d = x.shape
    assert norm_w.shape[0] == 1, "single-layer problem"
    blk = ATTN_BLOCK
    tm = min(INPROJ_ROWS, s)
    tm_out = min(OUTPROJ_ROWS, s)
    assert s % tm == 0 and s % blk == 0 and s % tm_out == 0

    w_in_b = w_in[0].astype(BF16)
    w_out_b = w_out[0].astype(BF16)
    plain_w = jnp.ones((1, (N_SEG - 2) * SEG), F32).at[0, 0:SEG].set(SB_SCALE * LOG2_E)
    qk_w = jnp.stack([jnp.tile(q_norm_w[0] * (DF_SCALE * LOG2_E), SEG // QK_DIM),
                      jnp.tile(k_norm_w[0], SEG // QK_DIM)])
    cos_t, s1_t, s2_t = _rope_tables(N_META + s)

    x2d = x.reshape(b * s, d)
    qk, plain = _inproj(x2d, norm_w, w_in_b, plain_w, qk_w, cos_t[N_META:], s1_t[N_META:],
                        s2_t[N_META:], rows_per_seq=s, tm=tm)
    qk, plain = qk.reshape(b, s, -1), plain.reshape(b, s, -1)
    meta_qk, meta_plain = _inproj(meta, norm_w, w_in_b, plain_w, qk_w, cos_t[:N_META],
                                  s1_t[:N_META], s2_t[:N_META], rows_per_seq=N_META,
                                  tm=N_META)

    tri = (jnp.arange(blk)[None, :] > jnp.arange(blk)[:, None]).astype(BF16)
    const2 = lambda bi, h, i: (0, 0)
    sb_mixed = _attention(_sb_kernel, (plain, 0), (plain, 1), (plain, 2), (plain, 3),
                          (meta_plain, 1), (meta_plain, 2), [tri],
                          [pl.BlockSpec((blk, blk), const2)],
                          [pltpu.VMEM((HEADS_PER_STEP, s // blk, HEAD, blk), BF16),
                           pltpu.VMEM((HEADS_PER_STEP, 1, blk), F32),
                           pltpu.VMEM((HEADS_PER_STEP, HEAD, blk), F32)],
                          blk=blk, heads=HEADS_PER_STEP, name="sb_attn")
    vec = pl.BlockSpec((1, QK_DIM), const2)
    df_mixed = _attention(_df_kernel, (qk, 0), (qk, 1), (plain, 4), (plain, 5),
                          (meta_qk, 1), (meta_plain, 4),
                          [lambda_q1, lambda_k1, lambda_q2, lambda_k2, subln_w],
                          [vec, vec, vec, vec, pl.BlockSpec((1, HEAD), const2)],
                          [pltpu.VMEM((HEADS_PER_STEP, 2 * blk, HEAD), BF16),
                           pltpu.VMEM((HEADS_PER_STEP, s // blk, HEAD + ONES_ROWS, blk), BF16),
                           pltpu.VMEM((HEADS_PER_STEP, 1, LANES), F32),
                           pltpu.VMEM((HEADS_PER_STEP, 1, 2 * blk), F32),
                           pltpu.VMEM((HEADS_PER_STEP, 1, 2 * blk), F32),
                           pltpu.VMEM((HEADS_PER_STEP, HEAD + ONES_ROWS, 2 * blk), F32)],
                          blk=blk, heads=HEADS_PER_STEP, name="df_attn")
    out = _outproj(sb_mixed.reshape(b * s, SEG), df_mixed.reshape(b * s, SEG), w_out_b, x2d,
                   tm=tm_out)
    return out.reshape(b, s, d)
```

```python
import functools
import math

import jax
import jax.numpy as jnp
from jax import lax
from jax.experimental import pallas as pl
from jax.experimental.pallas import tpu as pltpu

N_META = 16
HEAD = 128
N_HEADS = 8
SEG = N_HEADS * HEAD
N_SEG = 8
SEG_DF_Q, SEG_DF_K = 4, 5
QK_DIM = 64
ROT_HALF = 8
ROPE_THETA = 500000.0
RMS_EPS = 1e-6
SUBLN_EPS = 1e-5
LAMBDA_INIT = 0.8 - 0.6 * math.exp(-0.3 * 0)
SB_SCALE = 1.0 / math.sqrt(HEAD)
SB_DEAD_LOG2 = -130.0
DF_SCALE = 1.0 / math.sqrt(QK_DIM)
LOG2_E = math.log2(math.e)
ATTN_BLOCK = 256
INPROJ_ROWS = 1024
OUTPROJ_ROWS = 512
INPROJ_CHUNK = 256
ONES_ROWS = 16
HEADS_PER_STEP = N_HEADS
DF_SHIFT_MARGIN = 1.02
DF_FIXED_SHIFT_RANGE = 120.0

LANES = 128
V7X_VMEM_LIMIT = 56 * 1024 * 1024

F32 = jnp.float32
BF16 = jnp.bfloat16

_NT = (((1,), (1,)), ((), ()))
_TN = (((0,), (0,)), ((), ()))


def _inproj_kernel(x_ref, nw_ref, w_ref, cw_ref, qkw_ref, cos_ref, s1_ref, s2_ref,
                   oqk_ref, opl_ref, u_ref, raw_ref):
    j = pl.program_id(1)

    @pl.when(j == 0)
    def _():
        x = x_ref[...]
        ms = jnp.mean(x * x, axis=-1, keepdims=True)
        u_ref[...] = (x * lax.rsqrt(ms + RMS_EPS) * nw_ref[...]).astype(BF16)

    def chunks():
        for c in range(w_ref.shape[1] // INPROJ_CHUNK):
            cols = slice(c * INPROJ_CHUNK, (c + 1) * INPROJ_CHUNK)
            yield cols, jnp.dot(u_ref[...], w_ref[:, cols], preferred_element_type=F32)

    def park(slot):
        for cols, acc in chunks():
            raw_ref[slot, :, cols] = acc

    def plain():
        for cols, acc in chunks():
            opl_ref[:, cols] = (acc * cw_ref[:, cols]).astype(opl_ref.dtype)

    def qk_epilogue(slot):
        first = lax.broadcasted_iota(jnp.int32, (1, LANES), 1) < QK_DIM
        cos, s1, s2 = cos_ref[...], s1_ref[...], s2_ref[...]
        for c in range(raw_ref.shape[2] // LANES):
            lanes = slice(c * LANES, (c + 1) * LANES)
            y = raw_ref[slot, :, lanes]
            y2 = y * y
            sa = jnp.sum(jnp.where(first, y2, 0.0), axis=1, keepdims=True)
            sb = jnp.sum(jnp.where(first, 0.0, y2), axis=1, keepdims=True)
            ms = jnp.where(first, sa, sb) * (1.0 / QK_DIM)
            yn = y * lax.rsqrt(ms + RMS_EPS) * qkw_ref[slot:slot + 1, lanes]
            out = (yn * cos + pltpu.roll(yn, LANES - ROT_HALF, 1) * s1
                   + pltpu.roll(yn, ROT_HALF, 1) * s2)
            oqk_ref[:, lanes] = out.astype(oqk_ref.dtype)

    @pl.when(j == 0)
    def _():
        park(0)

    @pl.when(j == 1)
    def _():
        park(1)
        qk_epilogue(0)

    @pl.when(j == 2)
    def _():
        plain()
        qk_epilogue(1)

    @pl.when(j > 2)
    def _():
        plain()


def _inproj_segment(j):
    return jnp.where(j == 0, SEG_DF_Q, jnp.where(j == 1, SEG_DF_K,
                                                 jnp.where(j < SEG_DF_Q + 2, j - 2, j)))


def _inproj(x2d, norm_w, w_bf16, plain_w, qk_w, cos_t, s1_t, s2_t, *, rows_per_seq, tm):
    m, d = x2d.shape
    tn = SEG
    n_pos_tiles = rows_per_seq // tm
    pos_map = lambda i, j: (i % n_pos_tiles, 0)
    plain_map = lambda i, j: (i, jnp.maximum(j - 2, 0))
    return pl.pallas_call(
        _inproj_kernel,
        grid=(m // tm, N_SEG),
        in_specs=[
            pl.BlockSpec((tm, d), lambda i, j: (i, 0)),
            pl.BlockSpec((1, d), lambda i, j: (0, 0)),
            pl.BlockSpec((d, tn), lambda i, j: (0, _inproj_segment(j))),
            pl.BlockSpec((1, tn), lambda i, j: (0, jnp.maximum(j - 2, 0))),
            pl.BlockSpec((2, tn), lambda i, j: (0, 0)),
            pl.BlockSpec((tm, LANES), pos_map),
            pl.BlockSpec((tm, LANES), pos_map),
            pl.BlockSpec((tm, LANES), pos_map),
        ],
        out_specs=[pl.BlockSpec((tm, tn), lambda i, j: (i, jnp.where(j < 2, 0, 1))),
                   pl.BlockSpec((tm, tn), plain_map)],
        out_shape=[jax.ShapeDtypeStruct((m, 2 * SEG), BF16),
                   jax.ShapeDtypeStruct((m, (N_SEG - 2) * SEG), BF16)],
        scratch_shapes=[pltpu.VMEM((tm, d), BF16), pltpu.VMEM((2, tm, tn), F32)],
        compiler_params=pltpu.CompilerParams(
            dimension_semantics=("parallel", "arbitrary"),
            vmem_limit_bytes=V7X_VMEM_LIMIT),
        name="inproj",
    )(x2d, norm_w, w_bf16, plain_w, qk_w, cos_t, s1_t, s2_t)


def _silu(g):
    return g * (1.0 / (1.0 + jnp.exp(-g)))


def _head_cols(h):
    return slice(h * HEAD, (h + 1) * HEAD)


def _fill_vt(v_ref, vt_ref, *, blk, heads):
    for h in range(heads):
        for j in range(v_ref.shape[0] // blk):
            vt_ref[h, j, 0:HEAD] = v_ref[j * blk:(j + 1) * blk, _head_cols(h)].T


def _sb_kernel(q_ref, k_ref, v_ref, g_ref, km_ref, vm_ref, tri_ref, o_ref, vt_ref, carry_ref,
               acc_ref, *, blk, heads):
    i = pl.program_id(2)
    hs = range(heads)

    @pl.when(i == 0)
    def _():
        _fill_vt(v_ref, vt_ref, blk=blk, heads=heads)

    def blocks(parts):
        units = [(part, h) for part in parts for h in hs]
        zs = [lax.dot_general(keys(h), q_ref[:, _head_cols(h)], _NT, preferred_element_type=F32)
              for (keys, _, _, _), h in units]
        log_betas, log_keeps, his, los = [], [], [], []
        for z, ((_, _, _, valid), _) in zip(zs, units):
            sp = jnp.log2(1.0 + jnp.exp2(-jnp.abs(z)))
            log_beta = jnp.minimum(z, 0.0) - sp
            log_keep = log_beta - z
            if valid is not None:
                log_keep = jnp.where(valid, log_keep, 0.0)
            hi = log_keep.astype(BF16)
            log_betas.append(log_beta)
            log_keeps.append(log_keep)
            his.append(hi)
            los.append((log_keep - hi.astype(F32)).astype(BF16))
        tails = [jnp.dot(tri, his[u], preferred_element_type=F32)
                 + jnp.dot(tri, los[u], preferred_element_type=F32)
                 for u, ((_, _, tri, _), _) in enumerate(units)]
        carry = {h: carry_ref[h] for h in hs}
        probs = []
        for u, ((_, _, _, valid), h) in enumerate(units):
            a = jnp.exp2(log_betas[u] + tails[u] + carry[h])
            if valid is not None:
                a = jnp.where(valid, a, 0.0)
            probs.append(a.astype(BF16))
            carry[h] = carry[h] + tails[u][0:1, :] + log_keeps[u][0:1, :]
        upd = [weighted_values(h, probs[u]) for u, ((_, weighted_values, _, _), h)
               in enumerate(units)]
        for u, (_, h) in enumerate(units):
            acc_ref[h] += upd[u]
        for h in hs:
            carry_ref[h] = carry[h]

    def real_part(j, valid):
        return (lambda h: k_ref[pl.ds(pl.multiple_of(j * blk, blk), blk), _head_cols(h)],
                lambda h, a: jnp.dot(vt_ref[h, j], a, preferred_element_type=F32),
                tri_ref[...], valid)

    def live():
        return jnp.max(carry_ref[...]) > SB_DEAD_LOG2

    carry_ref[...] = jnp.zeros_like(carry_ref)
    acc_ref[...] = jnp.zeros_like(acc_ref)
    r = lax.broadcasted_iota(jnp.int32, (blk, blk), 0)
    c = lax.broadcasted_iota(jnp.int32, (blk, blk), 1)
    @pl.when(i == 0)
    def _():
        blocks([real_part(0, r < c)])

    @pl.when(i > 0)
    def _():
        blocks([real_part(i, r < c), real_part(i - 1, None)])

    def older(state):
        jj, _ = state
        blocks([real_part(i - 1 - jj, None)])
        return jj + 1, jnp.logical_and(jj + 1 < i, live())

    lax.while_loop(lambda state: state[1], older,
                   (jnp.int32(1), jnp.logical_and(i > 1, live())))

    @pl.when(live())
    def _():
        blocks([(lambda h: km_ref[:, _head_cols(h)],
                 lambda h, a: lax.dot_general(vm_ref[:, _head_cols(h)], a, _TN,
                                              preferred_element_type=F32),
                 tri_ref[0:N_META, 0:N_META], None)])

    for h in hs:
        g = g_ref[:, _head_cols(h)].astype(F32)
        o_ref[:, _head_cols(h)] = (acc_ref[h].T * _silu(g)).astype(o_ref.dtype)


def _df_kernel(q_ref, k_ref, v_ref, g_ref, km_ref, vm_ref, lq1_ref, lk1_ref, lq2_ref, lk2_ref,
               sw_ref, o_ref, q2_ref, vt_ref, kmax_ref, shift_ref, m_ref, acc_ref, *, blk, heads):
    i = pl.program_id(2)
    hs = range(heads)
    first = lax.broadcasted_iota(jnp.int32, (1, LANES), 1) < QK_DIM

    @pl.when(i == 0)
    def _():
        _fill_vt(v_ref, vt_ref, blk=blk, heads=heads)
        vt_ref[:, :, HEAD:, :] = jnp.ones_like(vt_ref[:, :, HEAD:, :])
        same = (lax.broadcasted_iota(jnp.int32, (LANES, LANES), 0) < QK_DIM) == first
        group_sum = jnp.where(same, 1.0, 0.0).astype(BF16)
        for h in hs:
            best = jnp.zeros((1, LANES), F32)
            for ref in (k_ref, km_ref):
                kf = ref[:, _head_cols(h)].astype(F32)
                norms = jnp.dot((kf * kf).astype(BF16), group_sum, preferred_element_type=F32)
                best = jnp.maximum(best, jnp.max(norms, axis=0, keepdims=True))
            kmax_ref[h] = best

    col = lax.broadcasted_iota(jnp.int32, (1, 2 * blk), 1)
    ones_rows = jnp.ones((8, LANES), BF16)
    for h in hs:
        qf = q_ref[:, _head_cols(h)].astype(F32)
        q2_ref[h, 0:blk, :] = jnp.where(first, qf, 0.0).astype(BF16)
        q2_ref[h, blk:2 * blk, :] = jnp.where(first, 0.0, qf).astype(BF16)
        q2f = q2_ref[h].astype(F32)
        qn2 = lax.dot_general(ones_rows, (q2f * q2f).astype(BF16), _NT,
                              preferred_element_type=F32)[0:1, :]
        kn2 = jnp.where(col < blk, kmax_ref[h, :, 0:1], kmax_ref[h, :, QK_DIM:QK_DIM + 1])
        shift_ref[h] = jnp.sqrt(qn2 * kn2) * DF_SHIFT_MARGIN
    acc_ref[...] = jnp.zeros_like(acc_ref)
    ones_meta = jnp.ones((N_META, ONES_ROWS), BF16)

    def block(j, own, fixed_shift):
        off = N_META if own else 0
        rows = pl.ds(pl.multiple_of(j * blk, blk), blk)

        def keys(h):
            k = k_ref[rows, _head_cols(h)]
            return jnp.concatenate([km_ref[:, _head_cols(h)], k], axis=0) if own else k

        if own:
            r = lax.broadcasted_iota(jnp.int32, (off + blk, 2 * blk), 0)
            c = lax.broadcasted_iota(jnp.int32, (off + blk, 2 * blk), 1)
            valid = r - off <= jnp.where(c >= blk, c - blk, c)
        scores, alphas, probs = {}, {}, {}

        def score_matmul(h):
            scores[h] = lax.dot_general(keys(h), q2_ref[h], _NT, preferred_element_type=F32)

        def softmax(h):
            s = jnp.where(valid, scores[h], -jnp.inf) if own else scores[h]
            if fixed_shift:
                probs[h] = jnp.exp2(s - shift_ref[h]).astype(BF16)
                return
            m_new = jnp.maximum(m_ref[h], jnp.max(s, axis=0, keepdims=True))
            alphas[h] = jnp.exp2(m_ref[h] - m_new)
            m_ref[h] = m_new
            probs[h] = jnp.exp2(s - m_new).astype(BF16)

        def value_matmul(h):
            u = jnp.dot(vt_ref[h, j], probs[h][off:off + blk], preferred_element_type=F32)
            if own:
                vm1 = jnp.concatenate([vm_ref[:, _head_cols(h)], ones_meta], axis=1)
                u = u + lax.dot_general(vm1, probs[h][0:off], _TN, preferred_element_type=F32)
            acc_ref[h] = acc_ref[h] + u if fixed_shift else alphas[h] * acc_ref[h] + u

        for stage in (score_matmul, softmax, value_matmul):
            for h in hs:
                stage(h)

    def all_blocks(fixed_shift):
        if fixed_shift:
            @pl.loop(0, i // 2)
            def _(t):
                block(2 * t, False, True)
                block(2 * t + 1, False, True)

            @pl.when(i % 2 == 1)
            def _():
                block(i - 1, False, True)
        else:
            @pl.loop(0, i)
            def _(t):
                block(t, False, False)

        block(i, True, fixed_shift)

    safe = 2.0 * jnp.max(shift_ref[...]) < DF_FIXED_SHIFT_RANGE

    @pl.when(safe)
    def _():
        all_blocks(True)

    @pl.when(jnp.logical_not(safe))
    def _():
        m_ref[...] = jnp.full_like(m_ref, -jnp.inf)
        all_blocks(False)

    lam = (jnp.exp(jnp.sum(lq1_ref[...] * lk1_ref[...], axis=1, keepdims=True))
           - jnp.exp(jnp.sum(lq2_ref[...] * lk2_ref[...], axis=1, keepdims=True))
           + LAMBDA_INIT)
    for h in hs:
        o = acc_ref[h, 0:HEAD] * (1.0 / acc_ref[h, HEAD:HEAD + 1])
        o = o[:, :blk] - lam * o[:, blk:]
        ms = jnp.mean(o * o, axis=0, keepdims=True)
        y = (o * lax.rsqrt(ms + SUBLN_EPS)).T * (sw_ref[...] * (1.0 - LAMBDA_INIT))
        g = g_ref[:, _head_cols(h)].astype(F32)
        o_ref[:, _head_cols(h)] = (y * _silu(g)).astype(o_ref.dtype)


def _attention(kernel_fn, q, k, v, g, meta_k, meta_v, extra, extra_specs, scratch, *, blk,
               heads, name):
    b, s, _ = q[0].shape
    groups = N_HEADS // heads
    width = heads * HEAD

    def col(seg):
        return lambda bi, h, i, seg=seg: (bi, i, seg * groups + h)

    def full(seg):
        return lambda bi, h, i, seg=seg: (bi, 0, seg * groups + h)

    def meta(seg):
        return lambda bi, h, i, seg=seg: (0, seg * groups + h)

    return pl.pallas_call(
        functools.partial(kernel_fn, blk=blk, heads=heads),
        grid=(b, groups, s // blk),
        in_specs=[
            pl.BlockSpec((None, blk, width), col(q[1])),
            pl.BlockSpec((None, s, width), full(k[1])),
            pl.BlockSpec((None, s, width), full(v[1])),
            pl.BlockSpec((None, blk, width), col(g[1])),
            pl.BlockSpec((N_META, width), meta(meta_k[1])),
            pl.BlockSpec((N_META, width), meta(meta_v[1])),
        ] + extra_specs,
        out_specs=pl.BlockSpec((None, blk, width), lambda bi, h, i: (bi, i, h)),
        out_shape=jax.ShapeDtypeStruct((b, s, SEG), BF16),
        scratch_shapes=scratch,
        compiler_params=pltpu.CompilerParams(
            dimension_semantics=("parallel", "parallel", "arbitrary"),
            vmem_limit_bytes=V7X_VMEM_LIMIT),
        name=name,
    )(q[0], k[0], v[0], g[0], meta_k[0], meta_v[0], *extra)


def _outproj_kernel(sb_ref, df_ref, wsb_ref, wdf_ref, x_ref, o_ref):
    y = (jnp.dot(sb_ref[...], wsb_ref[...], preferred_element_type=F32)
         + jnp.dot(df_ref[...], wdf_ref[...], preferred_element_type=F32))
    o_ref[...] = x_ref[...] + y


def _outproj(sb_mixed, df_mixed, w_out_bf16, x2d, *, tm):
    m, d = x2d.shape
    const = lambda i: (0, 0)
    return pl.pallas_call(
        _outproj_kernel,
        grid=(m // tm,),
        in_specs=[
            pl.BlockSpec((tm, SEG), lambda i: (i, 0)),
            pl.BlockSpec((tm, SEG), lambda i: (i, 0)),
            pl.BlockSpec((SEG, d), lambda i: (0, 0)),
            pl.BlockSpec((SEG, d), lambda i: (1, 0)),
            pl.BlockSpec((tm, d), lambda i: (i, 0)),
        ],
        out_specs=pl.BlockSpec((tm, d), lambda i: (i, 0)),
        out_shape=jax.ShapeDtypeStruct((m, d), x2d.dtype),
        compiler_params=pltpu.CompilerParams(
            dimension_semantics=("parallel",),
            vmem_limit_bytes=V7X_VMEM_LIMIT),
        name="outproj",
    )(sb_mixed, df_mixed, w_out_bf16, w_out_bf16, x2d)


def _rope_tables(t_all):
    pos = jnp.arange(t_all, dtype=F32)
    inv_freq = ROPE_THETA ** (-jnp.arange(0, 2 * ROT_HALF, 2, dtype=F32) / (2 * ROT_HALF))
    ang = pos[:, None] * inv_freq[None, :]
    cos, sin = jnp.cos(ang), jnp.sin(ang)
    lane = jnp.arange(LANES) % QK_DIM
    pair = lane % ROT_HALF
    cos_t = jnp.where(lane < 2 * ROT_HALF, cos[:, pair], 1.0)
    s1_t = jnp.where(lane < ROT_HALF, -sin[:, pair], 0.0)
    s2_t = jnp.where((lane >= ROT_HALF) & (lane < 2 * ROT_HALF), sin[:, pair], 0.0)
    return cos_t, s1_t, s2_t


def kernel(x, meta, norm_w, w_in, q_norm_w, k_norm_w, lambda_q1, lambda_k1, lambda_q2,
           lambda_k2, subln_w, w_out):
    b, s, d = x.shape
    assert norm_w.shape[0] == 1, "single-layer problem"
    blk = ATTN_BLOCK
    tm = min(INPROJ_ROWS, s)
    tm_out = min(OUTPROJ_ROWS, s)
    assert s % tm == 0 and s % blk == 0 and s % tm_out == 0

    w_in_b = w_in[0].astype(BF16)
    w_out_b = w_out[0].astype(BF16)
    plain_w = jnp.ones((1, (N_SEG - 2) * SEG), F32).at[0, 0:SEG].set(SB_SCALE * LOG2_E)
    qk_w = jnp.stack([jnp.tile(q_norm_w[0] * (DF_SCALE * LOG2_E), SEG // QK_DIM),
                      jnp.tile(k_norm_w[0], SEG // QK_DIM)])
    cos_t, s1_t, s2_t = _rope_tables(N_META + s)

    x2d = x.reshape(b * s, d)
    qk, plain = _inproj(x2d, norm_w, w_in_b, plain_w, qk_w, cos_t[N_META:], s1_t[N_META:],
                        s2_t[N_META:], rows_per_seq=s, tm=tm)
    qk, plain = qk.reshape(b, s, -1), plain.reshape(b, s, -1)
    meta_qk, meta_plain = _inproj(meta, norm_w, w_in_b, plain_w, qk_w, cos_t[:N_META],
                                  s1_t[:N_META], s2_t[:N_META], rows_per_seq=N_META,
                                  tm=N_META)

    tri = (jnp.arange(blk)[None, :] > jnp.arange(blk)[:, None]).astype(BF16)
    const2 = lambda bi, h, i: (0, 0)
    sb_mixed = _attention(_sb_kernel, (plain, 0), (plain, 1), (plain, 2), (plain, 3),
                          (meta_plain, 1), (meta_plain, 2), [tri],
                          [pl.BlockSpec((blk, blk), const2)],
                          [pltpu.VMEM((HEADS_PER_STEP, s // blk, HEAD, blk), BF16),
                           pltpu.VMEM((HEADS_PER_STEP, 1, blk), F32),
                           pltpu.VMEM((HEADS_PER_STEP, HEAD, blk), F32)],
                          blk=blk, heads=HEADS_PER_STEP, name="sb_attn")
    vec = pl.BlockSpec((1, QK_DIM), const2)
    df_mixed = _attention(_df_kernel, (qk, 0), (qk, 1), (plain, 4), (plain, 5),
                          (meta_qk, 1), (meta_plain, 4),
                          [lambda_q1, lambda_k1, lambda_q2, lambda_k2, subln_w],
                          [vec, vec, vec, vec, pl.BlockSpec((1, HEAD), const2)],
                          [pltpu.VMEM((HEADS_PER_STEP, 2 * blk, HEAD), BF16),
                           pltpu.VMEM((HEADS_PER_STEP, s // blk, HEAD + ONES_ROWS, blk), BF16),
                           pltpu.VMEM((HEADS_PER_STEP, 1, LANES), F32),
                           pltpu.VMEM((HEADS_PER_STEP, 1, 2 * blk), F32),
                           pltpu.VMEM((HEADS_PER_STEP, 1, 2 * blk), F32),
                           pltpu.VMEM((HEADS_PER_STEP, HEAD + ONES_ROWS, 2 * blk), F32)],
                          blk=blk, heads=HEADS_PER_STEP, name="df_attn")
    out = _outproj(sb_mixed.reshape(b * s, SEG), df_mixed.reshape(b * s, SEG), w_out_b, x2d,
                   tm=tm_out)
    return out.reshape(b, s, d)
```
